```python
import math
import jax, jax.numpy as jnp
from jax import lax
import numpy as np

D_MODEL = 1024
BATCH = 8
SEQ = 2048
DEPTH = 4
DEC_BATCH = 2
DEC_SEQ = 16384
PAST_LEN = 128

EPS = 1e-6
N_MIXERS = 3
N_A = (DEPTH + 2) // 3
N_B = (DEPTH + 1) // 3
N_C = DEPTH // 3

D_FF = 2816

A_HIDDEN = 6 * D_MODEL
A_HALF = A_HIDDEN // 2
A_CHUNK = 128
A_GROUPS = 8
A_GROUP_DIM = A_HALF // A_GROUPS

B_PATTERNS = ((128, 1), (512, 4), (2048, 16))
B_N_GROUPS = len(B_PATTERNS)
B_HEADS_PER_GROUP = 6
B_HEAD_DIM = 64
B_N_HEADS = B_N_GROUPS * B_HEADS_PER_GROUP
B_WIDTH = B_N_HEADS * B_HEAD_DIM
REL_BUCKETS = 32
REL_MAX_DIST = 1024
NEG_INF = -1e30

CONV_WIDTH = 31
CONV_PAD = CONV_WIDTH // 2

kernel_name = "hybrid_gmlp_dilatedattn_conformer_encoder"


def rms_norm(x, g):
    xf = x.astype(jnp.float32)
    y = xf * lax.rsqrt(jnp.mean(xf * xf, axis=-1, keepdims=True) + EPS)
    return (y * g.astype(jnp.float32)).astype(x.dtype)


def swiglu(x, w_gate, w_up, w_down):
    return (jax.nn.silu(x @ w_gate) * (x @ w_up)) @ w_down


def chunked_gmlp(x, w_in, g_v, w_spatial, b_spatial, w_out):
    B, S, _ = x.shape
    h = jax.nn.gelu(x @ w_in)
    u, v = h[..., :A_HALF], h[..., A_HALF:]
    v = rms_norm(v, g_v)
    v = v.reshape(B, S // A_CHUNK, A_CHUNK, A_GROUPS, A_GROUP_DIM)
    v = jnp.einsum('gpq,bcqgd->bcpgd', w_spatial, v) + b_spatial.T[None, None, :, :, None]
    v = v.reshape(B, S, A_HALF)
    return (u * v) @ w_out


def t5_bucket(rel):
    half = REL_BUCKETS // 2
    max_exact = half // 2
    ret = jnp.where(rel > 0, half, 0)
    n = jnp.abs(rel)
    nf = jnp.maximum(n, 1).astype(jnp.float32)
    large = max_exact + (jnp.log(nf / max_exact) / math.log(REL_MAX_DIST / max_exact)
                         * (half - max_exact)).astype(jnp.int32)
    large = jnp.minimum(large, half - 1)
    return ret + jnp.where(n < max_exact, n, large)


def dilated_band_attention(q, k, v, window, dil, bias_table):
    B, S, H, Dh = q.shape
    w_half = window // (2 * dil)
    blk = w_half
    L = S // dil
    nb = -(-L // blk)
    Lp = nb * blk

    def to_residue(t):
        t = t.reshape(B, L, dil, H, Dh).transpose(0, 2, 1, 3, 4)
        return jnp.pad(t, ((0, 0), (0, 0), (0, Lp - L), (0, 0), (0, 0)))

    def band(t):
        tp = jnp.pad(to_residue(t), ((0, 0), (0, 0), (blk, blk), (0, 0), (0, 0)))
        tp = tp.reshape(B, dil, nb + 2, blk, H, Dh)
        return jnp.concatenate([tp[:, :, :-2], tp[:, :, 1:-1], tp[:, :, 2:]], axis=3)

    qr = to_residue(q).reshape(B, dil, nb, blk, H, Dh)
    kb, vb = band(k), band(v)
    scores = jnp.einsum('brcqhd,brckhd->brcqhk', qr, kb,
                        preferred_element_type=jnp.float32) * (1.0 / math.sqrt(Dh))

    q_idx = jnp.arange(blk, dtype=jnp.int32)
    k_idx = jnp.arange(3 * blk, dtype=jnp.int32)
    rel = k_idx[None, :] - blk - q_idx[:, None]
    bias = bias_table[t5_bucket(rel * dil)].astype(jnp.float32)
    bias = bias.transpose(0, 2, 1)
    kpos = (jnp.arange(nb, dtype=jnp.int32)[:, None] - 1) * blk + k_idx[None, :]
    valid = (jnp.abs(rel) <= w_half)[None] & ((kpos >= 0) & (kpos < L))[:, None, :]
    scores = jnp.where(valid[None, None, :, :, None, :], scores + bias[None, None, None], NEG_INF)

    lse = jax.nn.logsumexp(scores, axis=-1)
    probs = jnp.exp(scores - lse[..., None]).astype(v.dtype)
    out = jnp.einsum('brcqhk,brckhd->brcqhd', probs, vb)

    out = out.reshape(B, dil, Lp, H, Dh)[:, :, :L].transpose(0, 2, 1, 3, 4).reshape(B, S, H, Dh)
    lse = lse.reshape(B, dil, Lp, H)[:, :, :L].transpose(0, 2, 1, 3).reshape(B, S, H)
    return out, lse


def dilated_attention_mixer(x, w_qkv, w_out, rel_bias):
    B, S, _ = x.shape
    qkv = (x @ w_qkv).reshape(B, S, 3, B_N_GROUPS, B_HEADS_PER_GROUP, B_HEAD_DIM)
    outs, lses = [], []
    for g, (window, dil) in enumerate(B_PATTERNS):
        table = rel_bias[:, g * B_HEADS_PER_GROUP:(g + 1) * B_HEADS_PER_GROUP]
        o, l = dilated_band_attention(qkv[:, :, 0, g], qkv[:, :, 1, g], qkv[:, :, 2, g], window, dil, table)
        outs.append(o)
        lses.append(l)
    out = jnp.stack(outs, axis=2)
    alpha = jax.nn.softmax(jnp.stack(lses, axis=2), axis=2)
    out = out * alpha[..., None].astype(out.dtype)
    return out.reshape(B, S, B_WIDTH) @ w_out


def conformer_conv(x, w_pw1, b_pw1, w_dw, b_dw, g_norm, w_pw2, b_pw2):
    h = x @ w_pw1 + b_pw1
    h = h[..., :D_MODEL] * jax.nn.sigmoid(h[..., D_MODEL:])
    h = lax.conv_general_dilated(h, w_dw[:, None, :].astype(h.dtype), window_strides=(1,),
                                 padding=[(CONV_PAD, CONV_PAD)],
                                 dimension_numbers=('NWC', 'WIO', 'NWC'),
                                 feature_group_count=D_MODEL) + b_dw
    h = jax.nn.silu(rms_norm(h, g_norm))
    return h @ w_pw2 + b_pw2


def _trunk(x, p):
    for i in range(DEPTH):
        h = rms_norm(x, p['norm_ffn1'][i])
        x = x + 0.5 * swiglu(h, p['ffn1_w_gate'][i], p['ffn1_w_up'][i], p['ffn1_w_down'][i])
        h = rms_norm(x, p['norm_mix'][i])
        kind, j = i % N_MIXERS, i // N_MIXERS
        if kind == 0:
            m = chunked_gmlp(h, p['a_w_in'][j], p['a_g_v'][j], p['a_w_spatial'][j],
                             p['a_b_spatial'][j], p['a_w_out'][j])
        elif kind == 1:
            m = dilated_attention_mixer(h, p['b_w_qkv'][j], p['b_w_out'][j], p['rel_bias'])
        else:
            m = conformer_conv(h, p['c_w_pw1'][j], p['c_b_pw1'][j], p['c_w_dw'][j], p['c_b_dw'][j],
                               p['c_g_norm'][j], p['c_w_pw2'][j], p['c_b_pw2'][j])
        x = x + m
        h = rms_norm(x, p['norm_ffn2'][i])
        x = x + 0.5 * swiglu(h, p['ffn2_w_gate'][i], p['ffn2_w_up'][i], p['ffn2_w_down'][i])
    return rms_norm(x, p['norm_final'])


def _normal(key, shape, scale):
    return jax.random.normal(key, shape, jnp.float32) * scale


def setup_inputs(seed: int = 0) -> dict:
    key = jax.random.key(seed)
    ks = jax.random.split(key, 32)
    gain = lambda k, shape: 1.0 + _normal(k, shape, 0.02)
    return {
        'x_prompt': _normal(ks[0], (BATCH, SEQ, D_MODEL), 1.0),
        'x_sample': _normal(ks[1], (DEC_BATCH, DEC_SEQ, D_MODEL), 1.0),
        'norm_ffn1': gain(ks[2], (DEPTH, D_MODEL)),
        'ffn1_w_gate': _normal(ks[3], (DEPTH, D_MODEL, D_FF), D_MODEL ** -0.5),
        'ffn1_w_up': _normal(ks[4], (DEPTH, D_MODEL, D_FF), D_MODEL ** -0.5),
        'ffn1_w_down': _normal(ks[5], (DEPTH, D_FF, D_MODEL), D_FF ** -0.5),
        'norm_mix': gain(ks[6], (DEPTH, D_MODEL)),
        'norm_ffn2': gain(ks[7], (DEPTH, D_MODEL)),
        'ffn2_w_gate': _normal(ks[8], (DEPTH, D_MODEL, D_FF), D_MODEL ** -0.5),
        'ffn2_w_up': _normal(ks[9], (DEPTH, D_MODEL, D_FF), D_MODEL ** -0.5),
        'ffn2_w_down': _normal(ks[10], (DEPTH, D_FF, D_MODEL), D_FF ** -0.5),
        'a_w_in': _normal(ks[11], (N_A, D_MODEL, A_HIDDEN), D_MODEL ** -0.5),
        'a_g_v': gain(ks[12], (N_A, A_HALF)),
        'a_w_spatial': _normal(ks[13], (N_A, A_GROUPS, A_CHUNK, A_CHUNK), A_CHUNK ** -0.5),
        'a_b_spatial': 1.0 + _normal(ks[14], (N_A, A_GROUPS, A_CHUNK), 0.02),
        'a_w_out': _normal(ks[15], (N_A, A_HALF, D_MODEL), A_HALF ** -0.5),
        'b_w_qkv': _normal(ks[16], (N_B, D_MODEL, 3 * B_WIDTH), D_MODEL ** -0.5),
        'b_w_out': _normal(ks[17], (N_B, B_WIDTH, D_MODEL), B_WIDTH ** -0.5),
        'rel_bias': _normal(ks[18], (REL_BUCKETS, B_N_HEADS), 0.5),
        'c_w_pw1': _normal(ks[19], (N_C, D_MODEL, 2 * D_MODEL), D_MODEL ** -0.5),
        'c_b_pw1': _normal(ks[20], (N_C, 2 * D_MODEL), 0.02),
        'c_w_dw': _normal(ks[21], (N_C, CONV_WIDTH, D_MODEL), CONV_WIDTH ** -0.5),
        'c_b_dw': _normal(ks[22], (N_C, D_MODEL), 0.02),
        'c_g_norm': gain(ks[23], (N_C, D_MODEL)),
        'c_w_pw2': _normal(ks[24], (N_C, D_MODEL, D_MODEL), D_MODEL ** -0.5),
        'c_b_pw2': _normal(ks[25], (N_C, D_MODEL), 0.02),
        'norm_final': gain(ks[26], (D_MODEL,)),
    }


def reference(x_prompt, x_sample, norm_ffn1, ffn1_w_gate, ffn1_w_up, ffn1_w_down, norm_mix,
              norm_ffn2, ffn2_w_gate, ffn2_w_up, ffn2_w_down, a_w_in, a_g_v, a_w_spatial,
              a_b_spatial, a_w_out, b_w_qkv, b_w_out, rel_bias, c_w_pw1, c_b_pw1, c_w_dw,
              c_b_dw, c_g_norm, c_w_pw2, c_b_pw2, norm_final):
    p = dict(norm_ffn1=norm_ffn1, ffn1_w_gate=ffn1_w_gate, ffn1_w_up=ffn1_w_up,
             ffn1_w_down=ffn1_w_down, norm_mix=norm_mix, norm_ffn2=norm_ffn2,
             ffn2_w_gate=ffn2_w_gate, ffn2_w_up=ffn2_w_up, ffn2_w_down=ffn2_w_down,
             a_w_in=a_w_in, a_g_v=a_g_v, a_w_spatial=a_w_spatial, a_b_spatial=a_b_spatial,
             a_w_out=a_w_out, b_w_qkv=b_w_qkv, b_w_out=b_w_out, rel_bias=rel_bias,
             c_w_pw1=c_w_pw1, c_b_pw1=c_b_pw1, c_w_dw=c_w_dw, c_b_dw=c_b_dw,
             c_g_norm=c_g_norm, c_w_pw2=c_w_pw2, c_b_pw2=c_b_pw2, norm_final=norm_final)
    y_prompt = _trunk(x_prompt, p)
    y_sample = _trunk(x_sample, p)
    return (y_prompt, y_sample)
```

```python
import functools
import math

import jax
import jax.numpy as jnp
from jax import lax
from jax.experimental import pallas as pl
from jax.experimental.pallas import tpu as pltpu

F32 = jnp.float32
BF16 = jnp.bfloat16

EPS = 1e-6
NEG_INF = -1e30

A_CHUNK = 128
A_GROUPS = 8
B_PATTERNS = ((128, 1), (512, 4), (2048, 16))
B_HEADS_PER_GROUP = 6
B_HEAD_DIM = 64
B_GROUP_WIDTH = B_HEADS_PER_GROUP * B_HEAD_DIM
B_W_HALF = 64
REL_BUCKETS = 32
REL_MAX_DIST = 1024
CONV_WIDTH = 31
CONV_PAD = CONV_WIDTH // 2
CONV_HALO = 16

V7X_VMEM_LIMIT_BYTES = 56 * 1024 * 1024
LANES = 128


def _params(n_grid_dims):
    return pltpu.CompilerParams(
        dimension_semantics=("arbitrary",) * n_grid_dims,
        vmem_limit_bytes=V7X_VMEM_LIMIT_BYTES,
    )


def _resident(shape):
    nd = len(shape)
    return pl.BlockSpec(shape, lambda *_: (0,) * nd, pipeline_mode=pl.Buffered(1))


def _rows(tm, width):
    return pl.BlockSpec((tm, width), lambda i: (i, 0))


def _rms(x, g):
    ms = jnp.mean(x * x, axis=-1, keepdims=True)
    return x * lax.rsqrt(ms + EPS) * g


def _dot(a, b):
    return jnp.dot(a, b, preferred_element_type=F32)


def _ffn_kernel(x_ref, g_ref, wg_ref, wu_ref, wd_ref, *rest, final_norm):
    if final_norm:
        gf_ref, o_ref = rest
    else:
        (o_ref,) = rest
    x = x_ref[...]
    h = _rms(x, g_ref[...]).astype(BF16)
    gate = _dot(h, wg_ref[...])
    up = _dot(h, wu_ref[...])
    act = (gate * jax.nn.sigmoid(gate) * up).astype(BF16)
    out = x + 0.5 * _dot(act, wd_ref[...])
    if final_norm:
        out = _rms(out, gf_ref[...])
    o_ref[...] = out


def _ffn(x, g, wg, wu, wd, g_final=None, *, tm=512):
    n, d = x.shape
    f = wg.shape[1]
    assert n % tm == 0
    ins = [x, g.reshape(1, d), wg.astype(BF16), wu.astype(BF16), wd.astype(BF16)]
    specs = [_rows(tm, d), _resident((1, d)), _resident((d, f)), _resident((d, f)), _resident((f, d))]
    if g_final is not None:
        ins.append(g_final.reshape(1, d))
        specs.append(_resident((1, d)))
    return pl.pallas_call(
        functools.partial(_ffn_kernel, final_norm=g_final is not None),
        grid=(n // tm,),
        in_specs=specs,
        out_specs=_rows(tm, d),
        out_shape=jax.ShapeDtypeStruct((n, d), F32),
        compiler_params=_params(1),
        name="ffn",
    )(*ins)


def _gmlp_kernel(x_ref, g_ref, win_ref, gv_ref, ws_ref, bs_ref, wout_ref, o_ref, uv_ref, *, tm):
    half = gv_ref.shape[1]
    gdim = half // A_GROUPS
    x = x_ref[...]
    h = _rms(x, g_ref[...]).astype(BF16)
    hh = jax.nn.gelu(_dot(h, win_ref[...]))
    u = hh[:, :half]
    v = _rms(hh[:, half:], gv_ref[...]).astype(BF16)
    for c in range(tm // A_CHUNK):
        rows = slice(c * A_CHUNK, (c + 1) * A_CHUNK)
        for g in range(A_GROUPS):
            cols = slice(g * gdim, (g + 1) * gdim)
            sv = _dot(ws_ref[g], v[rows, cols]) + bs_ref[g]
            uv_ref[rows, cols] = (u[rows, cols] * sv).astype(BF16)
    o_ref[...] = x + _dot(uv_ref[...], wout_ref[...])


def _gmlp(x, g, w_in, g_v, w_sp, b_sp, w_out, *, tm=256):
    n, d = x.shape
    hid = w_in.shape[1]
    half = hid // 2
    assert n % tm == 0 and tm % A_CHUNK == 0
    return pl.pallas_call(
        functools.partial(_gmlp_kernel, tm=tm),
        grid=(n // tm,),
        in_specs=[
            _rows(tm, d), _resident((1, d)), _resident((d, hid)), _resident((1, half)),
            _resident((A_GROUPS, A_CHUNK, A_CHUNK)), _resident((A_GROUPS, A_CHUNK, 1)),
            _resident((half, d)),
        ],
        out_specs=_rows(tm, d),
        out_shape=jax.ShapeDtypeStruct((n, d), F32),
        scratch_shapes=[pltpu.VMEM((tm, half), BF16)],
        compiler_params=_params(1),
        name="gmlp",
    )(x, g.reshape(1, d), w_in.astype(BF16), g_v.reshape(1, half), w_sp.astype(BF16),
      b_sp.reshape(A_GROUPS, A_CHUNK, 1), w_out.astype(BF16))


def _seq_pos(row0, n_first, s_first, s_second):
    in_first = row0 < n_first
    pos = jnp.where(in_first, row0 % s_first, (row0 - n_first) % s_second)
    return pos, jnp.where(in_first, s_first, s_second)


def _conv_kernel(xp_ref, xc_ref, xn_ref, g_ref, w1_ref, b1_ref, wdw_ref, bdw_ref, gn_ref, w2_ref, b2_ref,
                 o_ref, glu_ref, conv_ref, *, t, n_first, s_first, s_second):
    d = xc_ref.shape[1]
    ext = t + 2 * CONV_HALO
    pos0, seq_len = _seq_pos(pl.program_id(0) * t, n_first, s_first, s_second)
    xc = xc_ref[...]
    xa = jnp.concatenate([xp_ref[...], xc, xn_ref[...]], axis=0)
    h = _rms(xa, g_ref[...]).astype(BF16)
    p = _dot(h, w1_ref[...]) + b1_ref[...]
    glu = p[:, :d] * jax.nn.sigmoid(p[:, d:])
    pos = pos0 - CONV_HALO + lax.broadcasted_iota(jnp.int32, (ext, 1), 0)
    glu_ref[...] = jnp.where((pos >= 0) & (pos < seq_len), glu, 0.0)
    for cb in range(d // LANES):
        cols = slice(cb * LANES, (cb + 1) * LANES)
        acc = jnp.zeros((t, LANES), F32)
        for j in range(CONV_WIDTH):
            r0 = CONV_HALO - CONV_PAD + j
            acc = acc + wdw_ref[j:j + 1, cols] * glu_ref[r0:r0 + t, cols]
        conv_ref[:, cols] = acc
    c = conv_ref[...] + bdw_ref[...]
    c = _rms(c, gn_ref[...])
    c = (c * jax.nn.sigmoid(c)).astype(BF16)
    o_ref[...] = xc + _dot(c, w2_ref[...]) + b2_ref[...]


def _conformer(x, g, w1, b1, wdw, bdw, gn, w2, b2, *, n_first, s_first, s_second, t=256):
    n, d = x.shape
    assert n % t == 0 and s_first % t == 0 and s_second % t == 0 and t % CONV_HALO == 0
    hb = t // CONV_HALO
    n_hb = n // CONV_HALO
    kern = functools.partial(_conv_kernel, t=t, n_first=n_first, s_first=s_first, s_second=s_second)
    return pl.pallas_call(
        kern,
        grid=(n // t,),
        in_specs=[
            pl.BlockSpec((CONV_HALO, d), lambda i: (jnp.maximum(i * hb - 1, 0), 0)),
            _rows(t, d),
            pl.BlockSpec((CONV_HALO, d), lambda i: (jnp.minimum((i + 1) * hb, n_hb - 1), 0)),
            _resident((1, d)), _resident((d, 2 * d)), _resident((1, 2 * d)),
            _resident((CONV_WIDTH, d)), _resident((1, d)), _resident((1, d)),
            _resident((d, d)), _resident((1, d)),
        ],
        out_specs=_rows(t, d),
        out_shape=jax.ShapeDtypeStruct((n, d), F32),
        scratch_shapes=[pltpu.VMEM((t + 2 * CONV_HALO, d), F32), pltpu.VMEM((t, d), F32)],
        compiler_params=_params(1),
        name="conformer",
    )(x, x, x, g.reshape(1, d), w1.astype(BF16), b1.reshape(1, 2 * d), wdw, bdw.reshape(1, d),
      gn.reshape(1, d), w2.astype(BF16), b2.reshape(1, d))


def _qkv_kernel(x_ref, g_ref, w_ref, o_ref, *, q_width):
    h = _rms(x_ref[...], g_ref[...]).astype(BF16)
    qkv = _dot(h, w_ref[...])
    scale = 1.0 / math.sqrt(B_HEAD_DIM)
    o_ref[:, :q_width] = (qkv[:, :q_width] * scale).astype(BF16)
    o_ref[:, q_width:] = qkv[:, q_width:].astype(BF16)


def _qkv_proj(x, g, w, *, tm=512):
    n, d = x.shape
    width = w.shape[1]
    assert n % tm == 0
    return pl.pallas_call(
        functools.partial(_qkv_kernel, q_width=width // 3),
        grid=(n // tm,),
        in_specs=[_rows(tm, d), _resident((1, d)), _resident((d, width))],
        out_specs=_rows(tm, width),
        out_shape=jax.ShapeDtypeStruct((n, width), BF16),
        compiler_params=_params(1),
        name="qkv_proj",
    )(x, g.reshape(1, d), w.astype(BF16))


def _t5_bucket(rel):
    half = REL_BUCKETS // 2
    max_exact = half // 2
    ret = jnp.where(rel > 0, half, 0)
    n = jnp.abs(rel)
    nf = jnp.maximum(n, 1).astype(jnp.float32)
    large = max_exact + (jnp.log(nf / max_exact) / math.log(REL_MAX_DIST / max_exact)
                         * (half - max_exact)).astype(jnp.int32)
    large = jnp.minimum(large, half - 1)
    return ret + jnp.where(n < max_exact, n, large)


def _bias_kernel(table_ref, bucket_ref, rel_ref, o_ref):
    bucket = bucket_ref[...]
    in_band = jnp.abs(rel_ref[...]) <= B_W_HALF
    for h in range(o_ref.shape[0]):
        acc = jnp.zeros(bucket.shape, F32)
        for b in range(REL_BUCKETS):
            acc = jnp.where(bucket == b, table_ref[b, h], acc)
        o_ref[h] = jnp.where(in_band, acc, NEG_INF)


def _band_bias(table, dil, t):
    tk = t + 2 * B_W_HALF
    q_idx = jnp.arange(t, dtype=jnp.int32)
    k_idx = jnp.arange(tk, dtype=jnp.int32)
    rel = k_idx[None, :] - B_W_HALF - q_idx[:, None]
    bucket = _t5_bucket(rel * dil)
    nh = table.shape[1]
    return pl.pallas_call(
        _bias_kernel,
        in_specs=[pl.BlockSpec(memory_space=pltpu.SMEM), pl.BlockSpec((t, tk), lambda: (0, 0)),
                  pl.BlockSpec((t, tk), lambda: (0, 0))],
        out_specs=pl.BlockSpec((nh, t, tk), lambda: (0, 0, 0)),
        out_shape=jax.ShapeDtypeStruct((nh, t, tk), F32),
        name="band_bias",
    )(table, bucket, rel)


def _attn_kernel(q_ref, kp_ref, kc_ref, kn_ref, vp_ref, vc_ref, vn_ref, bias_ref, o_ref, lse_ref,
                 *, t, rows_first, l_first, l_second):
    tk = t + 2 * B_W_HALF
    pos0, seq_len = _seq_pos(pl.program_id(1) * t, rows_first, l_first, l_second)
    kpos = pos0 - B_W_HALF + lax.broadcasted_iota(jnp.int32, (1, tk), 1)
    edge = jnp.where((kpos >= 0) & (kpos < seq_len), 0.0, NEG_INF)
    k = jnp.concatenate([kp_ref[...], kc_ref[...], kn_ref[...]], axis=0)
    v = jnp.concatenate([vp_ref[...], vc_ref[...], vn_ref[...]], axis=0)
    q = q_ref[...]
    lane = lax.broadcasted_iota(jnp.int32, (1, LANES), 1)
    low = lane < B_HEAD_DIM
    for pair in range(B_GROUP_WIDTH // LANES):
        cols = slice(pair * LANES, (pair + 1) * LANES)
        qp, kp, vp = q[:, cols], k[:, cols], v[:, cols]
        outs, lses = [], []
        for sub in range(2):
            qh = jnp.where(low if sub == 0 else ~low, qp, jnp.zeros_like(qp))
            s = lax.dot_general(qh, kp, (((1,), (1,)), ((), ())), preferred_element_type=F32)
            s = s + bias_ref[2 * pair + sub] + edge
            m = jnp.max(s, axis=-1, keepdims=True)
            p = jnp.exp(s - m)
            l = jnp.sum(p, axis=-1, keepdims=True)
            outs.append(_dot(p.astype(BF16), vp) * (1.0 / l))
            lses.append(m + jnp.log(l))
        o_ref[:, cols] = jnp.where(low, outs[0], outs[1])
        lse_ref[:, cols] = jnp.where(low, lses[0], lses[1])


def _band_attention(qkv, bias, g, dil, *, n_first, s_first, s_second):
    n, width = qkv.shape
    gw = B_GROUP_WIDTH
    n_blk = width // gw
    l_first, l_second = s_first // dil, s_second // dil
    t = min(256, l_first)
    assert l_first % t == 0 and l_second % t == 0 and t % B_W_HALF == 0
    rows = n // dil
    hb = t // B_W_HALF
    n_hb = rows // B_W_HALF
    view = qkv.reshape(rows, dil * width)
    bias = bias[:, :t, :t + 2 * B_W_HALF]

    def cur(which):
        return pl.BlockSpec((t, gw), lambda r, i: (i, r * n_blk + 3 * which + g))

    def prev(which):
        return pl.BlockSpec((B_W_HALF, gw), lambda r, i: (jnp.maximum(i * hb - 1, 0), r * n_blk + 3 * which + g))

    def nxt(which):
        return pl.BlockSpec((B_W_HALF, gw),
                            lambda r, i: (jnp.minimum((i + 1) * hb, n_hb - 1), r * n_blk + 3 * which + g))

    kern = functools.partial(_attn_kernel, t=t, rows_first=n_first // dil, l_first=l_first, l_second=l_second)
    out_spec = pl.BlockSpec((t, gw), lambda r, i: (i, r))
    out, lse = pl.pallas_call(
        kern,
        grid=(dil, rows // t),
        in_specs=[cur(0), prev(1), cur(1), nxt(1), prev(2), cur(2), nxt(2),
                  pl.BlockSpec(bias.shape, lambda r, i: (0, 0, 0), pipeline_mode=pl.Buffered(1))],
        out_specs=[out_spec, out_spec],
        out_shape=[jax.ShapeDtypeStruct((rows, dil * gw), F32)] * 2,
        compiler_params=_params(2),
        name=f"band_attn_d{dil}",
    )(view, view, view, view, view, view, view, bias)
    return out.reshape(n, gw), lse.reshape(n, gw)


def _combine_kernel(x_ref, o0_ref, o1_ref, o2_ref, l0_ref, l1_ref, l2_ref, w_ref, out_ref):
    l0, l1, l2 = l0_ref[...], l1_ref[...], l2_ref[...]
    m = jnp.maximum(jnp.maximum(l0, l1), l2)
    e0, e1, e2 = jnp.exp(l0 - m), jnp.exp(l1 - m), jnp.exp(l2 - m)
    inv = 1.0 / (e0 + e1 + e2)
    acc = x_ref[...]
    for g, (o_ref, e) in enumerate(((o0_ref, e0), (o1_ref, e1), (o2_ref, e2))):
        acc = acc + _dot((o_ref[...] * (e * inv)).astype(BF16), w_ref[g])
    out_ref[...] = acc


def _combine_proj(x, outs, lses, w_out, *, tm=512):
    n, d = x.shape
    gw = B_GROUP_WIDTH
    ng = len(outs)
    assert n % tm == 0
    return pl.pallas_call(
        _combine_kernel,
        grid=(n // tm,),
        in_specs=[_rows(tm, d)] + [_rows(tm, gw)] * (2 * ng) + [_resident((ng, gw, d))],
        out_specs=_rows(tm, d),
        out_shape=jax.ShapeDtypeStruct((n, d), F32),
        compiler_params=_params(1),
        name="attn_combine",
    )(x, *outs, *lses, w_out.astype(BF16).reshape(ng, gw, d))


def _dilated_attention(x, g, w_qkv, w_out, rel_bias, *, n_first, s_first, s_second):
    qkv = _qkv_proj(x, g, w_qkv)
    outs, lses = [], []
    for grp, (window, dil) in enumerate(B_PATTERNS):
        assert window // (2 * dil) == B_W_HALF
        table = rel_bias[:, grp * B_HEADS_PER_GROUP:(grp + 1) * B_HEADS_PER_GROUP]
        bias = _band_bias(table, dil, 256)
        o, l = _band_attention(qkv, bias, grp, dil, n_first=n_first, s_first=s_first, s_second=s_second)
        outs.append(o)
        lses.append(l)
    return _combine_proj(x, outs, lses, w_out)


def kernel(x_prompt, x_sample, norm_ffn1, ffn1_w_gate, ffn1_w_up, ffn1_w_down, norm_mix, norm_ffn2, ffn2_w_gate, ffn2_w_up, ffn2_w_down, a_w_in, a_g_v, a_w_spatial, a_b_spatial, a_w_out, b_w_qkv, b_w_out, rel_bias, c_w_pw1, c_b_pw1, c_w_dw, c_b_dw, c_g_norm, c_w_pw2, c_b_pw2, norm_final):
    bp, sp, d = x_prompt.shape
    bs, ss, _ = x_sample.shape
    n_first = bp * sp
    depth = norm_ffn1.shape[0]
    seq = dict(n_first=n_first, s_first=sp, s_second=ss)
    x = jnp.concatenate([x_prompt.reshape(n_first, d), x_sample.reshape(bs * ss, d)], axis=0)
    for i in range(depth):
        x = _ffn(x, norm_ffn1[i], ffn1_w_gate[i], ffn1_w_up[i], ffn1_w_down[i])
        kind, j = i % 3, i // 3
        if kind == 0:
            x = _gmlp(x, norm_mix[i], a_w_in[j], a_g_v[j], a_w_spatial[j], a_b_spatial[j], a_w_out[j])
        elif kind == 1:
            x = _dilated_attention(x, norm_mix[i], b_w_qkv[j], b_w_out[j], rel_bias, **seq)
        else:
            x = _conformer(x, norm_mix[i], c_w_pw1[j], c_b_pw1[j], c_w_dw[j], c_b_dw[j], c_g_norm[j],
                           c_w_pw2[j], c_b_pw2[j], **seq)
        x = _ffn(x, norm_ffn2[i], ffn2_w_gate[i], ffn2_w_up[i], ffn2_w_down[i],
                 norm_final if i == depth - 1 else None)
    return x[:n_first].reshape(bp, sp, d), x[n_first:].reshape(bs, ss, d)
```

```python
import functools
import math

import jax
import jax.numpy as jnp
from jax import lax
from jax.experimental import pallas as pl
from jax.experimental.pallas import tpu as pltpu

F32 = jnp.float32
BF16 = jnp.bfloat16

EPS = 1e-6
NEG_INF = -1e30

A_CHUNK = 128
A_GROUPS = 8
B_PATTERNS = ((128, 1), (512, 4), (2048, 16))
B_HEADS_PER_GROUP = 6
B_HEAD_DIM = 64
B_GROUP_WIDTH = B_HEADS_PER_GROUP * B_HEAD_DIM
B_W_HALF = 64
REL_BUCKETS = 32
REL_MAX_DIST = 1024
CONV_WIDTH = 31
CONV_PAD = CONV_WIDTH // 2
CONV_HALO = 16

V7X_VMEM_LIMIT_BYTES = 56 * 1024 * 1024
LANES = 128
SUBLANES = 8
CONV_ROWS = 128


def _params(n_grid_dims):
    return pltpu.CompilerParams(
        dimension_semantics=("arbitrary",) * n_grid_dims,
        vmem_limit_bytes=V7X_VMEM_LIMIT_BYTES,
    )


def _resident(shape):
    nd = len(shape)
    return pl.BlockSpec(shape, lambda *_: (0,) * nd, pipeline_mode=pl.Buffered(1))


def _rows(tm, width):
    return pl.BlockSpec((tm, width), lambda i: (i, 0))


def _rms(x, g):
    ms = jnp.mean(x * x, axis=-1, keepdims=True)
    return x * lax.rsqrt(ms + EPS) * g


def _dot(a, b):
    return jnp.dot(a, b, preferred_element_type=F32)


def _ffn_kernel(x_ref, g_ref, wg_ref, wu_ref, wd_ref, *rest, final_norm):
    if final_norm:
        gf_ref, o_ref = rest
    else:
        (o_ref,) = rest
    x = x_ref[...]
    h = _rms(x, g_ref[...]).astype(BF16)
    gate = _dot(h, wg_ref[...])
    up = _dot(h, wu_ref[...])
    act = (gate * jax.nn.sigmoid(gate) * up).astype(BF16)
    out = x + 0.5 * _dot(act, wd_ref[...])
    if final_norm:
        out = _rms(out, gf_ref[...])
    o_ref[...] = out


def _ffn(x, g, wg, wu, wd, g_final=None, *, tm=512):
    n, d = x.shape
    f = wg.shape[1]
    assert n % tm == 0
    ins = [x, g.reshape(1, d), wg.astype(BF16), wu.astype(BF16), wd.astype(BF16)]
    specs = [_rows(tm, d), _resident((1, d)), _resident((d, f)), _resident((d, f)), _resident((f, d))]
    if g_final is not None:
        ins.append(g_final.reshape(1, d))
        specs.append(_resident((1, d)))
    return pl.pallas_call(
        functools.partial(_ffn_kernel, final_norm=g_final is not None),
        grid=(n // tm,),
        in_specs=specs,
        out_specs=_rows(tm, d),
        out_shape=jax.ShapeDtypeStruct((n, d), F32),
        compiler_params=_params(1),
        name="ffn",
    )(*ins)


def _gmlp_kernel(x_ref, g_ref, win_ref, gv_ref, ws_ref, bs_ref, wout_ref, o_ref, uv_ref, *, tm):
    half = gv_ref.shape[1]
    gdim = half // A_GROUPS
    x = x_ref[...]
    h = _rms(x, g_ref[...]).astype(BF16)
    hh = jax.nn.gelu(_dot(h, win_ref[...]))
    u = hh[:, :half]
    v = _rms(hh[:, half:], gv_ref[...]).astype(BF16)
    for c in range(tm // A_CHUNK):
        rows = slice(c * A_CHUNK, (c + 1) * A_CHUNK)
        for g in range(A_GROUPS):
            cols = slice(g * gdim, (g + 1) * gdim)
            sv = _dot(ws_ref[g], v[rows, cols]) + bs_ref[g]
            uv_ref[rows, cols] = (u[rows, cols] * sv).astype(BF16)
    o_ref[...] = x + _dot(uv_ref[...], wout_ref[...])


def _gmlp(x, g, w_in, g_v, w_sp, b_sp, w_out, *, tm=256):
    n, d = x.shape
    hid = w_in.shape[1]
    half = hid // 2
    assert n % tm == 0 and tm % A_CHUNK == 0
    return pl.pallas_call(
        functools.partial(_gmlp_kernel, tm=tm),
        grid=(n // tm,),
        in_specs=[
            _rows(tm, d), _resident((1, d)), _resident((d, hid)), _resident((1, half)),
            _resident((A_GROUPS, A_CHUNK, A_CHUNK)), _resident((A_GROUPS, A_CHUNK, 1)),
            _resident((half, d)),
        ],
        out_specs=_rows(tm, d),
        out_shape=jax.ShapeDtypeStruct((n, d), F32),
        scratch_shapes=[pltpu.VMEM((tm, half), BF16)],
        compiler_params=_params(1),
        name="gmlp",
    )(x, g.reshape(1, d), w_in.astype(BF16), g_v.reshape(1, half), w_sp.astype(BF16),
      b_sp.reshape(A_GROUPS, A_CHUNK, 1), w_out.astype(BF16))


def _seq_pos(row0, n_first, s_first, s_second):
    in_first = row0 < n_first
    pos = jnp.where(in_first, row0 % s_first, (row0 - n_first) % s_second)
    return pos, jnp.where(in_first, s_first, s_second)


def _conv_kernel(xp_ref, xc_ref, xn_ref, g_ref, w1_ref, b1_ref, wdw_ref, bdw_ref, gn_ref, w2_ref, b2_ref,
                 o_ref, glu_ref, conv_ref, shift_ref, *, t, n_first, s_first, s_second):
    d = xc_ref.shape[1]
    ext = t + 2 * CONV_HALO
    pos0, seq_len = _seq_pos(pl.program_id(0) * t, n_first, s_first, s_second)
    xc = xc_ref[...]
    xa = jnp.concatenate([xp_ref[...], xc, xn_ref[...]], axis=0)
    h = _rms(xa, g_ref[...]).astype(BF16)
    p = _dot(h, w1_ref[...]) + b1_ref[...]
    glu = p[:, :d] * jax.nn.sigmoid(p[:, d:])
    pos = pos0 - CONV_HALO + lax.broadcasted_iota(jnp.int32, (ext, 1), 0)
    glu_ref[...] = jnp.where((pos >= 0) & (pos < seq_len), glu, 0.0)
    base = CONV_HALO - CONV_PAD
    span = ((CONV_WIDTH - 1) // SUBLANES) * SUBLANES
    for cb in range(d // LANES):
        cols = slice(cb * LANES, (cb + 1) * LANES)
        for rb in range(t // CONV_ROWS):
            for b in range(SUBLANES):
                r0 = rb * CONV_ROWS + base + b
                shift_ref[b] = glu_ref[r0:r0 + CONV_ROWS + span, cols]
            acc = jnp.zeros((CONV_ROWS, LANES), F32)
            for j in range(CONV_WIDTH):
                b, off = j % SUBLANES, j - j % SUBLANES
                acc = acc + wdw_ref[j:j + 1, cols] * shift_ref[b, off:off + CONV_ROWS, :]
            conv_ref[rb * CONV_ROWS:(rb + 1) * CONV_ROWS, cols] = acc
    c = conv_ref[...] + bdw_ref[...]
    c = _rms(c, gn_ref[...])
    c = (c * jax.nn.sigmoid(c)).astype(BF16)
    o_ref[...] = xc + _dot(c, w2_ref[...]) + b2_ref[...]


def _conformer(x, g, w1, b1, wdw, bdw, gn, w2, b2, *, n_first, s_first, s_second, t=256):
    n, d = x.shape
    assert n % t == 0 and s_first % t == 0 and s_second % t == 0 and t % CONV_HALO == 0
    hb = t // CONV_HALO
    n_hb = n // CONV_HALO
    kern = functools.partial(_conv_kernel, t=t, n_first=n_first, s_first=s_first, s_second=s_second)
    return pl.pallas_call(
        kern,
        grid=(n // t,),
        in_specs=[
            pl.BlockSpec((CONV_HALO, d), lambda i: (jnp.maximum(i * hb - 1, 0), 0)),
            _rows(t, d),
            pl.BlockSpec((CONV_HALO, d), lambda i: (jnp.minimum((i + 1) * hb, n_hb - 1), 0)),
            _resident((1, d)), _resident((d, 2 * d)), _resident((1, 2 * d)),
            _resident((CONV_WIDTH, d)), _resident((1, d)), _resident((1, d)),
            _resident((d, d)), _resident((1, d)),
        ],
        out_specs=_rows(t, d),
        out_shape=jax.ShapeDtypeStruct((n, d), F32),
        scratch_shapes=[pltpu.VMEM((t + 2 * CONV_HALO, d), F32), pltpu.VMEM((t, d), F32),
                        pltpu.VMEM((SUBLANES, CONV_ROWS + (CONV_WIDTH - 1) // SUBLANES * SUBLANES, LANES), F32)],
        compiler_params=_params(1),
        name="conformer",
    )(x, x, x, g.reshape(1, d), w1.astype(BF16), b1.reshape(1, 2 * d), wdw, bdw.reshape(1, d),
      gn.reshape(1, d), w2.astype(BF16), b2.reshape(1, d))


def _qkv_kernel(x_ref, g_ref, w_ref, *rest, tm):
    o_refs, h_ref = rest[:-1], rest[-1]
    hn = _rms(x_ref[...], g_ref[...])
    n_cb = h_ref.shape[0]
    for cb in range(n_cb):
        h_ref[cb] = hn[:, cb * LANES:(cb + 1) * LANES]
    gw = B_GROUP_WIDTH
    scale = 1.0 / math.sqrt(B_HEAD_DIM)
    for grp, (o_ref, (_, dil)) in enumerate(zip(o_refs, B_PATTERNS)):
        rows = tm // dil
        if dil == 1:
            h = hn
        else:
            h = jnp.concatenate(
                [jnp.concatenate([h_ref[cb, pl.ds(r, rows, stride=dil), :] for cb in range(n_cb)], axis=1)
                 for r in range(dil)], axis=0)
        qkv = _dot(h.astype(BF16), w_ref[grp])
        for r in range(dil):
            blk = qkv[r * rows:(r + 1) * rows]
            o_ref[r, :, :gw] = (blk[:, :gw] * scale).astype(BF16)
            o_ref[r, :, gw:] = blk[:, gw:].astype(BF16)


def _qkv_proj(x, g, w, *, tm=512):
    n, d = x.shape
    ng = len(B_PATTERNS)
    gw = B_GROUP_WIDTH
    assert n % tm == 0 and w.shape[1] == 3 * ng * gw
    w_grp = w.astype(BF16).reshape(d, 3, ng, gw).transpose(2, 0, 1, 3).reshape(ng, d, 3 * gw)
    return pl.pallas_call(
        functools.partial(_qkv_kernel, tm=tm),
        grid=(n // tm,),
        in_specs=[_rows(tm, d), _resident((1, d)), _resident((ng, d, 3 * gw))],
        out_specs=[pl.BlockSpec((dil, tm // dil, 3 * gw), lambda i: (0, i, 0)) for _, dil in B_PATTERNS],
        out_shape=[jax.ShapeDtypeStruct((dil, n // dil, 3 * gw), BF16) for _, dil in B_PATTERNS],
        scratch_shapes=[pltpu.VMEM((d // LANES, tm, LANES), F32)],
        compiler_params=_params(1),
        name="qkv_proj",
    )(x, g.reshape(1, d), w_grp)


def _t5_bucket(rel):
    half = REL_BUCKETS // 2
    max_exact = half // 2
    ret = jnp.where(rel > 0, half, 0)
    n = jnp.abs(rel)
    nf = jnp.maximum(n, 1).astype(jnp.float32)
    large = max_exact + (jnp.log(nf / max_exact) / math.log(REL_MAX_DIST / max_exact)
                         * (half - max_exact)).astype(jnp.int32)
    large = jnp.minimum(large, half - 1)
    return ret + jnp.where(n < max_exact, n, large)


def _bias_kernel(table_ref, bucket_ref, rel_ref, o_ref):
    bucket = bucket_ref[...]
    in_band = jnp.abs(rel_ref[...]) <= B_W_HALF
    for h in range(o_ref.shape[0]):
        acc = jnp.zeros(bucket.shape, F32)
        for b in range(REL_BUCKETS):
            acc = jnp.where(bucket == b, table_ref[b, h], acc)
        o_ref[h] = jnp.where(in_band, acc, NEG_INF)


def _band_bias(table, dil, t):
    tk = t + 2 * B_W_HALF
    q_idx = jnp.arange(t, dtype=jnp.int32)
    k_idx = jnp.arange(tk, dtype=jnp.int32)
    rel = k_idx[None, :] - B_W_HALF - q_idx[:, None]
    bucket = _t5_bucket(rel * dil)
    nh = table.shape[1]
    return pl.pallas_call(
        _bias_kernel,
        in_specs=[pl.BlockSpec(memory_space=pltpu.SMEM), pl.BlockSpec((t, tk), lambda: (0, 0)),
                  pl.BlockSpec((t, tk), lambda: (0, 0))],
        out_specs=pl.BlockSpec((nh, t, tk), lambda: (0, 0, 0)),
        out_shape=jax.ShapeDtypeStruct((nh, t, tk), F32),
        name="band_bias",
    )(table, bucket, rel)


def _attn_kernel(q_ref, kp_ref, kc_ref, kn_ref, vp_ref, vc_ref, vn_ref, bias_ref, o_ref, lse_ref,
                 *, t, rows_first, l_first, l_second):
    tk = t + 2 * B_W_HALF
    pos0, seq_len = _seq_pos(pl.program_id(1) * t, rows_first, l_first, l_second)
    kpos = pos0 - B_W_HALF + lax.broadcasted_iota(jnp.int32, (1, tk), 1)
    edge = jnp.where((kpos >= 0) & (kpos < seq_len), 0.0, NEG_INF)
    k = jnp.concatenate([kp_ref[...], kc_ref[...], kn_ref[...]], axis=0)
    v = jnp.concatenate([vp_ref[...], vc_ref[...], vn_ref[...]], axis=0)
    q = q_ref[...]
    lane = lax.broadcasted_iota(jnp.int32, (1, LANES), 1)
    low = lane < B_HEAD_DIM
    for pair in range(B_GROUP_WIDTH // LANES):
        cols = slice(pair * LANES, (pair + 1) * LANES)
        qp, kp, vp = q[:, cols], k[:, cols], v[:, cols]
        outs, lses = [], []
        for sub in range(2):
            qh = jnp.where(low if sub == 0 else ~low, qp, jnp.zeros_like(qp))
            s = lax.dot_general(qh, kp, (((1,), (1,)), ((), ())), preferred_element_type=F32)
            s = s + bias_ref[2 * pair + sub] + edge
            m = jnp.max(s, axis=-1, keepdims=True)
            p = jnp.exp(s - m)
            l = jnp.sum(p, axis=-1, keepdims=True)
            outs.append(_dot(p.astype(BF16), vp) * (1.0 / l))
            lses.append(m + jnp.log(l))
        o_ref[:, cols] = jnp.where(low, outs[0], outs[1])
        lse_ref[:, cols] = jnp.where(low, lses[0], lses[1])


def _band_attention(qkv, bias, *, n_first, s_first, s_second):
    dil, rows, _ = qkv.shape
    gw = B_GROUP_WIDTH
    l_first, l_second = s_first // dil, s_second // dil
    t = min(256, l_first)
    assert l_first % t == 0 and l_second % t == 0 and t % B_W_HALF == 0
    hb = t // B_W_HALF
    n_hb = rows // B_W_HALF
    bias = bias[:, :t, :t + 2 * B_W_HALF]

    def cur(which):
        return pl.BlockSpec((None, t, gw), lambda r, i: (r, i, which))

    def prev(which):
        return pl.BlockSpec((None, B_W_HALF, gw), lambda r, i: (r, jnp.maximum(i * hb - 1, 0), which))

    def nxt(which):
        return pl.BlockSpec((None, B_W_HALF, gw), lambda r, i: (r, jnp.minimum((i + 1) * hb, n_hb - 1), which))

    kern = functools.partial(_attn_kernel, t=t, rows_first=n_first // dil, l_first=l_first, l_second=l_second)
    out_spec = pl.BlockSpec((None, t, gw), lambda r, i: (r, i, 0))
    return pl.pallas_call(
        kern,
        grid=(dil, rows // t),
        in_specs=[cur(0), prev(1), cur(1), nxt(1), prev(2), cur(2), nxt(2),
                  pl.BlockSpec(bias.shape, lambda r, i: (0, 0, 0), pipeline_mode=pl.Buffered(1))],
        out_specs=[out_spec, out_spec],
        out_shape=[jax.ShapeDtypeStruct((dil, rows, gw), F32)] * 2,
        compiler_params=_params(2),
        name=f"band_attn_d{dil}",
    )(qkv, qkv, qkv, qkv, qkv, qkv, qkv, bias)


def _combine_kernel(x_ref, o0_ref, o1_ref, o2_ref, l0_ref, l1_ref, l2_ref, w_ref, out_ref, o_scr, l_scr):
    n_cb = o_scr.shape[1]
    for grp, (o_ref, l_ref) in enumerate(((o0_ref, l0_ref), (o1_ref, l1_ref), (o2_ref, l2_ref))):
        dil, rows, _ = o_ref.shape
        for r in range(dil):
            for cb in range(n_cb):
                cols = slice(cb * LANES, (cb + 1) * LANES)
                o_scr[grp, cb, pl.ds(r, rows, stride=dil), :] = o_ref[r, :, cols]
                l_scr[grp, cb, pl.ds(r, rows, stride=dil), :] = l_ref[r, :, cols]

    def full(scr, grp):
        return jnp.concatenate([scr[grp, cb] for cb in range(n_cb)], axis=1)

    l0, l1, l2 = full(l_scr, 0), full(l_scr, 1), full(l_scr, 2)
    m = jnp.maximum(jnp.maximum(l0, l1), l2)
    e0, e1, e2 = jnp.exp(l0 - m), jnp.exp(l1 - m), jnp.exp(l2 - m)
    inv = 1.0 / (e0 + e1 + e2)
    acc = x_ref[...]
    for grp, e in enumerate((e0, e1, e2)):
        acc = acc + _dot((full(o_scr, grp) * (e * inv)).astype(BF16), w_ref[grp])
    out_ref[...] = acc


def _combine_proj(x, outs, lses, w_out, *, tm=512):
    n, d = x.shape
    gw = B_GROUP_WIDTH
    ng = len(outs)
    assert n % tm == 0
    grouped = [pl.BlockSpec((o.shape[0], tm // o.shape[0], gw), lambda i: (0, i, 0)) for o in outs]
    return pl.pallas_call(
        _combine_kernel,
        grid=(n // tm,),
        in_specs=[_rows(tm, d)] + grouped + grouped + [_resident((ng, gw, d))],
        out_specs=_rows(tm, d),
        out_shape=jax.ShapeDtypeStruct((n, d), F32),
        scratch_shapes=[pltpu.VMEM((ng, gw // LANES, tm, LANES), F32)] * 2,
        compiler_params=_params(1),
        name="attn_combine",
    )(x, *outs, *lses, w_out.astype(BF16).reshape(ng, gw, d))


def _dilated_attention(x, g, w_qkv, w_out, rel_bias, *, n_first, s_first, s_second):
    qkvs = _qkv_proj(x, g, w_qkv)
    outs, lses = [], []
    for grp, (window, dil) in enumerate(B_PATTERNS):
        assert window // (2 * dil) == B_W_HALF
        table = rel_bias[:, grp * B_HEADS_PER_GROUP:(grp + 1) * B_HEADS_PER_GROUP]
        bias = _band_bias(table, dil, 256)
        o, l = _band_attention(qkvs[grp], bias, n_first=n_first, s_first=s_first, s_second=s_second)
        outs.append(o)
        lses.append(l)
    return _combine_proj(x, outs, lses, w_out)


def kernel(x_prompt, x_sample, norm_ffn1, ffn1_w_gate, ffn1_w_up, ffn1_w_down, norm_mix, norm_ffn2, ffn2_w_gate, ffn2_w_up, ffn2_w_down, a_w_in, a_g_v, a_w_spatial, a_b_spatial, a_w_out, b_w_qkv, b_w_out, rel_bias, c_w_pw1, c_b_pw1, c_w_dw, c_b_dw, c_g_norm, c_w_pw2, c_b_pw2, norm_final):
    bp, sp, d = x_prompt.shape
    bs, ss, _ = x_sample.shape
    n_first = bp * sp
    depth = norm_ffn1.shape[0]
    seq = dict(n_first=n_first, s_first=sp, s_second=ss)
    x = jnp.concatenate([x_prompt.reshape(n_first, d), x_sample.reshape(bs * ss, d)], axis=0)
    for i in range(depth):
        x = _ffn(x, norm_ffn1[i], ffn1_w_gate[i], ffn1_w_up[i], ffn1_w_down[i])
        kind, j = i % 3, i // 3
        if kind == 0:
            x = _gmlp(x, norm_mix[i], a_w_in[j], a_g_v[j], a_w_spatial[j], a_b_spatial[j], a_w_out[j])
        elif kind == 1:
            x = _dilated_attention(x, norm_mix[i], b_w_qkv[j], b_w_out[j], rel_bias, **seq)
        else:
            x = _conformer(x, norm_mix[i], c_w_pw1[j], c_b_pw1[j], c_w_dw[j], c_b_dw[j], c_g_norm[j],
                           c_w_pw2[j], c_b_pw2[j], **seq)
        x = _ffn(x, norm_ffn2[i], ffn2_w_gate[i], ffn2_w_up[i], ffn2_w_down[i],
                 norm_final if i == depth - 1 else None)
    return x[:n_first].reshape(bp, sp, d), x[n_first:].reshape(bs, ss, d)
```

```python
import functools
import math

import jax
import jax.numpy as jnp
from jax import lax
from jax.experimental import pallas as pl
from jax.experimental.pallas import tpu as pltpu

F32 = jnp.float32
BF16 = jnp.bfloat16

EPS = 1e-6
NEG_INF = -1e30

A_CHUNK = 128
A_GROUPS = 8
A_BLOCK_GROUPS = 2
B_PATTERNS = ((128, 1), (512, 4), (2048, 16))
B_HEADS_PER_GROUP = 6
B_HEAD_DIM = 64
B_GROUP_WIDTH = B_HEADS_PER_GROUP * B_HEAD_DIM
B_W_HALF = 64
B_QUERY_BLOCK = 128
B_MAX_TILE = 512
REL_BUCKETS = 32
REL_MAX_DIST = 1024
CONV_WIDTH = 31
CONV_PAD = CONV_WIDTH // 2
CONV_HALO = 16

V7X_VMEM_LIMIT_BYTES = 56 * 1024 * 1024
LANES = 128
SUBLANES = 8
CONV_ROWS = 128
FFN_SUBTILES = 2


def _params(n_grid_dims):
    return pltpu.CompilerParams(
        dimension_semantics=("arbitrary",) * n_grid_dims,
        vmem_limit_bytes=V7X_VMEM_LIMIT_BYTES,
    )


def _resident(shape):
    nd = len(shape)
    return pl.BlockSpec(shape, lambda *_: (0,) * nd, pipeline_mode=pl.Buffered(1))


def _rows(tm, width):
    return pl.BlockSpec((tm, width), lambda i: (i, 0))


def _rms(x, g):
    ms = jnp.mean(x * x, axis=-1, keepdims=True)
    return x * lax.rsqrt(ms + EPS) * g


def _dot(a, b):
    return jnp.dot(a, b, preferred_element_type=F32)


def _ffn_kernel(*refs, n_x, tiles_first, final_norm):
    x_refs, refs = refs[:n_x], refs[n_x:]
    g_ref, wg_ref, wu_ref, wd_ref = refs[:4]
    gf_ref = refs[4] if final_norm else None
    o_ref = refs[-1]
    sub = x_refs[0].shape[0] // FFN_SUBTILES
    for s in range(FFN_SUBTILES):
        rows = slice(s * sub, (s + 1) * sub)
        x = x_refs[0][rows, :]
        if n_x == 2:
            x = jnp.where(pl.program_id(0) < tiles_first, x, x_refs[1][rows, :])
        h = _rms(x, g_ref[...]).astype(BF16)
        gate = _dot(h, wg_ref[...])
        up = _dot(h, wu_ref[...])
        act = (gate * jax.nn.sigmoid(gate) * up).astype(BF16)
        out = x + 0.5 * _dot(act, wd_ref[...])
        if final_norm:
            out = _rms(out, gf_ref[...])
        o_ref[rows, :] = out


def _layer_weight(shape, layer):
    return pl.BlockSpec((None,) + tuple(shape[1:]), lambda i: (layer,) + (0,) * (len(shape) - 1),
                        pipeline_mode=pl.Buffered(1))


def _ffn(xs, g, wg, wu, wd, layer, g_final=None, *, row_range=None, tm=512):
    d = xs[0].shape[1]
    n_rows = [x.shape[0] for x in xs]
    assert all(r % tm == 0 for r in n_rows)
    tiles_first = n_rows[0] // tm
    if row_range is not None:
        start, n = row_range
        assert len(xs) == 1 and start % tm == 0 and n % tm == 0
        x_specs = [pl.BlockSpec((tm, d), lambda i: (i + start // tm, 0))]
    elif len(xs) == 1:
        n = n_rows[0]
        x_specs = [_rows(tm, d)]
    else:
        n = sum(n_rows)
        x_specs = [pl.BlockSpec((tm, d), lambda i: (jnp.minimum(i, tiles_first - 1), 0)),
                   pl.BlockSpec((tm, d), lambda i: (jnp.maximum(i - tiles_first, 0), 0))]
    ins = list(xs) + [g.reshape(1, d), wg, wu, wd]
    specs = x_specs + [_resident((1, d)), _layer_weight(wg.shape, layer), _layer_weight(wu.shape, layer),
                       _layer_weight(wd.shape, layer)]
    if g_final is not None:
        ins.append(g_final.reshape(1, d))
        specs.append(_resident((1, d)))
    kern = functools.partial(_ffn_kernel, n_x=len(xs), tiles_first=tiles_first, final_norm=g_final is not None)
    return pl.pallas_call(
        kern,
        grid=(n // tm,),
        in_specs=specs,
        out_specs=_rows(tm, d),
        out_shape=jax.ShapeDtypeStruct((n, d), F32),
        compiler_params=_params(1),
        name="ffn",
    )(*ins)


def _gmlp_kernel(x_ref, g_ref, win_ref, gv_ref, ws_ref, bs_ref, wout_ref, o_ref, v_ref, uv_ref, *, tm):
    half = gv_ref.shape[1]
    gdim = half // A_GROUPS
    width = A_BLOCK_GROUPS * gdim
    x = x_ref[...]
    h = _rms(x, g_ref[...]).astype(BF16)
    ssq = jnp.zeros((tm, 1), F32)
    for blk in range(half // width):
        cols = slice(blk * width, (blk + 1) * width)
        v = jax.nn.gelu(_dot(h, win_ref[:, half + blk * width:half + (blk + 1) * width]))
        ssq = ssq + jnp.sum(v * v, axis=-1, keepdims=True)
        v_ref[:, cols] = v
    inv = lax.rsqrt(ssq / half + EPS)
    acc = x
    for blk in range(half // width):
        cols = slice(blk * width, (blk + 1) * width)
        u = jax.nn.gelu(_dot(h, win_ref[:, cols]))
        v = (v_ref[:, cols] * inv * gv_ref[:, cols]).astype(BF16)
        for c in range(tm // A_CHUNK):
            rows = slice(c * A_CHUNK, (c + 1) * A_CHUNK)
            for g in range(A_BLOCK_GROUPS):
                gc = slice(g * gdim, (g + 1) * gdim)
                sv = _dot(ws_ref[blk * A_BLOCK_GROUPS + g], v[rows, gc]) + bs_ref[blk * A_BLOCK_GROUPS + g]
                uv_ref[blk, rows, gc] = (u[rows, gc] * sv).astype(BF16)
        acc = acc + _dot(uv_ref[blk], wout_ref[cols, :])
    o_ref[...] = acc


def _gmlp(x, g, w_in, g_v, w_sp, b_sp, w_out, *, tm=512):
    n, d = x.shape
    hid = w_in.shape[1]
    half = hid // 2
    assert n % tm == 0 and tm % A_CHUNK == 0
    return pl.pallas_call(
        functools.partial(_gmlp_kernel, tm=tm),
        grid=(n // tm,),
        in_specs=[
            _rows(tm, d), _resident((1, d)), _resident((d, hid)), _resident((1, half)),
            _resident((A_GROUPS, A_CHUNK, A_CHUNK)), _resident((A_GROUPS, A_CHUNK, 1)),
            _resident((half, d)),
        ],
        out_specs=_rows(tm, d),
        out_shape=jax.ShapeDtypeStruct((n, d), F32),
        scratch_shapes=[pltpu.VMEM((tm, half), F32),
                        pltpu.VMEM((A_GROUPS // A_BLOCK_GROUPS, tm, A_BLOCK_GROUPS * (half // A_GROUPS)), BF16)],
        compiler_params=_params(1),
        name="gmlp",
    )(x, g.reshape(1, d), w_in.astype(BF16), g_v.reshape(1, half), w_sp.astype(BF16),
      b_sp.reshape(A_GROUPS, A_CHUNK, 1), w_out.astype(BF16))


def _seq_pos(row0, n_first, s_first, s_second):
    in_first = row0 < n_first
    pos = jnp.where(in_first, row0 % s_first, (row0 - n_first) % s_second)
    return pos, jnp.where(in_first, s_first, s_second)


def _conv_kernel(xp_ref, xc_ref, xn_ref, g_ref, w1_ref, b1_ref, wdw_ref, bdw_ref, gn_ref, w2_ref, b2_ref,
                 o_ref, glu_ref, conv_ref, shift_ref, *, t, n_first, s_first, s_second):
    d = xc_ref.shape[1]
    ext = t + 2 * CONV_HALO
    pos0, seq_len = _seq_pos(pl.program_id(0) * t, n_first, s_first, s_second)
    xc = xc_ref[...]
    xa = jnp.concatenate([xp_ref[...], xc, xn_ref[...]], axis=0)
    h = _rms(xa, g_ref[...]).astype(BF16)
    p = _dot(h, w1_ref[...]) + b1_ref[...]
    glu = p[:, :d] * jax.nn.sigmoid(p[:, d:])
    pos = pos0 - CONV_HALO + lax.broadcasted_iota(jnp.int32, (ext, 1), 0)
    glu_ref[...] = jnp.where((pos >= 0) & (pos < seq_len), glu, 0.0)
    base = CONV_HALO - CONV_PAD
    span = ((CONV_WIDTH - 1) // SUBLANES) * SUBLANES
    for cb in range(d // LANES):
        cols = slice(cb * LANES, (cb + 1) * LANES)
        for rb in range(t // CONV_ROWS):
            for b in range(SUBLANES):
                r0 = rb * CONV_ROWS + base + b
                shift_ref[b] = glu_ref[r0:r0 + CONV_ROWS + span, cols]
            acc = jnp.zeros((CONV_ROWS, LANES), F32)
            for j in range(CONV_WIDTH):
                b, off = j % SUBLANES, j - j % SUBLANES
                acc = acc + wdw_ref[j:j + 1, cols] * shift_ref[b, off:off + CONV_ROWS, :]
            conv_ref[rb * CONV_ROWS:(rb + 1) * CONV_ROWS, cols] = acc
    c = conv_ref[...] + bdw_ref[...]
    c = _rms(c, gn_ref[...])
    c = (c * jax.nn.sigmoid(c)).astype(BF16)
    o_ref[...] = xc + _dot(c, w2_ref[...]) + b2_ref[...]


def _conformer(x, g, w1, b1, wdw, bdw, gn, w2, b2, *, n_first, s_first, s_second, t=256):
    n, d = x.shape
    assert n % t == 0 and s_first % t == 0 and s_second % t == 0 and t % CONV_HALO == 0
    hb = t // CONV_HALO
    n_hb = n // CONV_HALO
    kern = functools.partial(_conv_kernel, t=t, n_first=n_first, s_first=s_first, s_second=s_second)
    return pl.pallas_call(
        kern,
        grid=(n // t,),
        in_specs=[
            pl.BlockSpec((CONV_HALO, d), lambda i: (jnp.maximum(i * hb - 1, 0), 0)),
            _rows(t, d),
            pl.BlockSpec((CONV_HALO, d), lambda i: (jnp.minimum((i + 1) * hb, n_hb - 1), 0)),
            _resident((1, d)), _resident((d, 2 * d)), _resident((1, 2 * d)),
            _resident((CONV_WIDTH, d)), _resident((1, d)), _resident((1, d)),
            _resident((d, d)), _resident((1, d)),
        ],
        out_specs=_rows(t, d),
        out_shape=jax.ShapeDtypeStruct((n, d), F32),
        scratch_shapes=[pltpu.VMEM((t + 2 * CONV_HALO, d), F32), pltpu.VMEM((t, d), F32),
                        pltpu.VMEM((SUBLANES, CONV_ROWS + (CONV_WIDTH - 1) // SUBLANES * SUBLANES, LANES), F32)],
        compiler_params=_params(1),
        name="conformer",
    )(x, x, x, g.reshape(1, d), w1.astype(BF16), b1.reshape(1, 2 * d), wdw, bdw.reshape(1, d),
      gn.reshape(1, d), w2.astype(BF16), b2.reshape(1, d))


def _qkv_kernel(x_ref, g_ref, w_ref, *rest, tm):
    o_refs, h_ref = rest[:-1], rest[-1]
    hn = _rms(x_ref[...], g_ref[...])
    n_cb = h_ref.shape[0]
    for cb in range(n_cb):
        h_ref[cb] = hn[:, cb * LANES:(cb + 1) * LANES]
    gw = B_GROUP_WIDTH
    scale = 1.0 / math.sqrt(B_HEAD_DIM)
    for grp, (o_ref, (_, dil)) in enumerate(zip(o_refs, B_PATTERNS)):
        rows = tm // dil
        if dil == 1:
            h = hn
        else:
            h = jnp.concatenate(
                [jnp.concatenate([h_ref[cb, pl.ds(r, rows, stride=dil), :] for cb in range(n_cb)], axis=1)
                 for r in range(dil)], axis=0)
        qkv = _dot(h.astype(BF16), w_ref[grp])
        for r in range(dil):
            blk = qkv[r * rows:(r + 1) * rows]
            o_ref[r, :, :gw] = (blk[:, :gw] * scale).astype(BF16)
            o_ref[r, :, gw:] = blk[:, gw:].astype(BF16)


def _qkv_proj(x, g, w, *, tm=512):
    n, d = x.shape
    ng = len(B_PATTERNS)
    gw = B_GROUP_WIDTH
    assert n % tm == 0 and w.shape[1] == 3 * ng * gw
    w_grp = w.astype(BF16).reshape(d, 3, ng, gw).transpose(2, 0, 1, 3).reshape(ng, d, 3 * gw)
    return pl.pallas_call(
        functools.partial(_qkv_kernel, tm=tm),
        grid=(n // tm,),
        in_specs=[_rows(tm, d), _resident((1, d)), _resident((ng, d, 3 * gw))],
        out_specs=[pl.BlockSpec((dil, tm // dil, 3 * gw), lambda i: (0, i, 0)) for _, dil in B_PATTERNS],
        out_shape=[jax.ShapeDtypeStruct((dil, n // dil, 3 * gw), BF16) for _, dil in B_PATTERNS],
        scratch_shapes=[pltpu.VMEM((d // LANES, tm, LANES), F32)],
        compiler_params=_params(1),
        name="qkv_proj",
    )(x, g.reshape(1, d), w_grp)


def _t5_bucket(rel):
    half = REL_BUCKETS // 2
    max_exact = half // 2
    ret = jnp.where(rel > 0, half, 0)
    n = jnp.abs(rel)
    nf = jnp.maximum(n, 1).astype(jnp.float32)
    large = max_exact + (jnp.log(nf / max_exact) / math.log(REL_MAX_DIST / max_exact)
                         * (half - max_exact)).astype(jnp.int32)
    large = jnp.minimum(large, half - 1)
    return ret + jnp.where(n < max_exact, n, large)


def _bias_kernel(table_ref, bucket_ref, rel_ref, o_ref):
    bucket = bucket_ref[...]
    tk = bucket.shape[1]
    in_band = jnp.abs(rel_ref[...]) <= B_W_HALF
    kk = lax.broadcasted_iota(jnp.int32, bucket.shape, 1)
    after_start, before_end = kk >= B_W_HALF, kk < tk - B_W_HALF
    masks = [in_band, in_band & after_start, in_band & before_end, in_band & after_start & before_end]
    for h in range(o_ref.shape[1]):
        acc = jnp.zeros(bucket.shape, F32)
        for b in range(REL_BUCKETS):
            acc = jnp.where(bucket == b, table_ref[b, h], acc)
        for e, mask in enumerate(masks):
            o_ref[e, h] = jnp.where(mask, acc, NEG_INF)


def _band_bias(table, dil):
    t = B_QUERY_BLOCK
    tk = t + 2 * B_W_HALF
    q_idx = jnp.arange(t, dtype=jnp.int32)
    k_idx = jnp.arange(tk, dtype=jnp.int32)
    rel = k_idx[None, :] - B_W_HALF - q_idx[:, None]
    bucket = _t5_bucket(rel * dil)
    nh = table.shape[1]
    return pl.pallas_call(
        _bias_kernel,
        in_specs=[pl.BlockSpec(memory_space=pltpu.SMEM), pl.BlockSpec((t, tk), lambda: (0, 0)),
                  pl.BlockSpec((t, tk), lambda: (0, 0))],
        out_specs=pl.BlockSpec((4, nh, t, tk), lambda: (0, 0, 0, 0)),
        out_shape=jax.ShapeDtypeStruct((4, nh, t, tk), F32),
        name="band_bias",
    )(table, bucket, rel)


def _attn_kernel(q_ref, kp_ref, kc_ref, kn_ref, vp_ref, vc_ref, vn_ref, bias_ref, o_ref, lse_ref,
                 *, t, rows_first, l_first, l_second):
    qb = B_QUERY_BLOCK
    n_qb = t // qb
    pos0, seq_len = _seq_pos(pl.program_id(1) * t, rows_first, l_first, l_second)
    at_start = (pos0 == 0).astype(jnp.int32)
    at_end = (pos0 + t == seq_len).astype(jnp.int32)
    lane = lax.broadcasted_iota(jnp.int32, (1, LANES), 1)
    low = lane < B_HEAD_DIM
    for res in range(q_ref.shape[0]):
        k = jnp.concatenate([kp_ref[res], kc_ref[res], kn_ref[res]], axis=0)
        v = jnp.concatenate([vp_ref[res], vc_ref[res], vn_ref[res]], axis=0)
        q = q_ref[res]
        for j in range(n_qb):
            edge = (at_start if j == 0 else 0) + (2 * at_end if j == n_qb - 1 else 0)
            q_rows = slice(j * qb, (j + 1) * qb)
            k_rows = slice(j * qb, (j + 1) * qb + 2 * B_W_HALF)
            for pair in range(B_GROUP_WIDTH // LANES):
                cols = slice(pair * LANES, (pair + 1) * LANES)
                qp, kp, vp = q[q_rows, cols], k[k_rows, cols], v[k_rows, cols]
                outs, lses = [], []
                for sub in range(2):
                    qh = jnp.where(low if sub == 0 else ~low, qp, jnp.zeros_like(qp))
                    s = lax.dot_general(qh, kp, (((1,), (1,)), ((), ())), preferred_element_type=F32)
                    s = s + bias_ref[edge, 2 * pair + sub]
                    m = jnp.max(s, axis=-1, keepdims=True)
                    p = jnp.exp(s - m)
                    l = jnp.sum(p, axis=-1, keepdims=True)
                    outs.append(_dot(p.astype(BF16), vp) * (1.0 / l))
                    lses.append(m + jnp.log(l))
                o_ref[res, q_rows, cols] = jnp.where(low, outs[0], outs[1])
                lse_ref[res, q_rows, cols] = jnp.where(low, lses[0], lses[1])


def _band_attention(qkv, bias, *, n_first, s_first, s_second):
    dil, rows, _ = qkv.shape
    gw = B_GROUP_WIDTH
    l_first, l_second = s_first // dil, s_second // dil
    t = min(B_MAX_TILE, l_first)
    assert l_first % t == 0 and l_second % t == 0 and t % B_QUERY_BLOCK == 0
    n_res = min(dil, B_MAX_TILE // t)
    assert dil % n_res == 0
    hb = t // B_W_HALF
    n_hb = rows // B_W_HALF

    def cur(which):
        return pl.BlockSpec((n_res, t, gw), lambda r, i: (r, i, which))

    def prev(which):
        return pl.BlockSpec((n_res, B_W_HALF, gw), lambda r, i: (r, jnp.maximum(i * hb - 1, 0), which))

    def nxt(which):
        return pl.BlockSpec((n_res, B_W_HALF, gw), lambda r, i: (r, jnp.minimum((i + 1) * hb, n_hb - 1), which))

    kern = functools.partial(_attn_kernel, t=t, rows_first=n_first // dil, l_first=l_first, l_second=l_second)
    out_spec = pl.BlockSpec((n_res, t, gw), lambda r, i: (r, i, 0))
    return pl.pallas_call(
        kern,
        grid=(dil // n_res, rows // t),
        in_specs=[cur(0), prev(1), cur(1), nxt(1), prev(2), cur(2), nxt(2),
                  pl.BlockSpec(bias.shape, lambda r, i: (0, 0, 0, 0), pipeline_mode=pl.Buffered(1))],
        out_specs=[out_spec, out_spec],
        out_shape=[jax.ShapeDtypeStruct((dil, rows, gw), F32)] * 2,
        compiler_params=_params(2),
        name=f"band_attn_d{dil}",
    )(qkv, qkv, qkv, qkv, qkv, qkv, qkv, bias)


def _combine_kernel(x_ref, o0_ref, o1_ref, o2_ref, l0_ref, l1_ref, l2_ref, w_ref, out_ref, o_scr, l_scr):
    n_cb = o_scr.shape[1]
    for grp, (o_ref, l_ref) in enumerate(((o0_ref, l0_ref), (o1_ref, l1_ref), (o2_ref, l2_ref))):
        dil, rows, _ = o_ref.shape
        for r in range(dil):
            for cb in range(n_cb):
                cols = slice(cb * LANES, (cb + 1) * LANES)
                o_scr[grp, cb, pl.ds(r, rows, stride=dil), :] = o_ref[r, :, cols]
                l_scr[grp, cb, pl.ds(r, rows, stride=dil), :] = l_ref[r, :, cols]

    def full(scr, grp):
        return jnp.concatenate([scr[grp, cb] for cb in range(n_cb)], axis=1)

    l0, l1, l2 = full(l_scr, 0), full(l_scr, 1), full(l_scr, 2)
    m = jnp.maximum(jnp.maximum(l0, l1), l2)
    e0, e1, e2 = jnp.exp(l0 - m), jnp.exp(l1 - m), jnp.exp(l2 - m)
    inv = 1.0 / (e0 + e1 + e2)
    acc = x_ref[...]
    for grp, e in enumerate((e0, e1, e2)):
        acc = acc + _dot((full(o_scr, grp) * (e * inv)).astype(BF16), w_ref[grp])
    out_ref[...] = acc


def _combine_proj(x, outs, lses, w_out, *, tm=512):
    n, d = x.shape
    gw = B_GROUP_WIDTH
    ng = len(outs)
    assert n % tm == 0
    grouped = [pl.BlockSpec((o.shape[0], tm // o.shape[0], gw), lambda i: (0, i, 0)) for o in outs]
    return pl.pallas_call(
        _combine_kernel,
        grid=(n // tm,),
        in_specs=[_rows(tm, d)] + grouped + grouped + [_resident((ng, gw, d))],
        out_specs=_rows(tm, d),
        out_shape=jax.ShapeDtypeStruct((n, d), F32),
        scratch_shapes=[pltpu.VMEM((ng, gw // LANES, tm, LANES), F32)] * 2,
        compiler_params=_params(1),
        name="attn_combine",
    )(x, *outs, *lses, w_out.astype(BF16).reshape(ng, gw, d))


def _dilated_attention(x, g, w_qkv, w_out, rel_bias, *, n_first, s_first, s_second):
    qkvs = _qkv_proj(x, g, w_qkv)
    outs, lses = [], []
    for grp, (window, dil) in enumerate(B_PATTERNS):
        assert window // (2 * dil) == B_W_HALF
        table = rel_bias[:, grp * B_HEADS_PER_GROUP:(grp + 1) * B_HEADS_PER_GROUP]
        bias = _band_bias(table, dil)
        o, l = _band_attention(qkvs[grp], bias, n_first=n_first, s_first=s_first, s_second=s_second)
        outs.append(o)
        lses.append(l)
    return _combine_proj(x, outs, lses, w_out)


def kernel(x_prompt, x_sample, norm_ffn1, ffn1_w_gate, ffn1_w_up, ffn1_w_down, norm_mix, norm_ffn2, ffn2_w_gate, ffn2_w_up, ffn2_w_down, a_w_in, a_g_v, a_w_spatial, a_b_spatial, a_w_out, b_w_qkv, b_w_out, rel_bias, c_w_pw1, c_b_pw1, c_w_dw, c_b_dw, c_g_norm, c_w_pw2, c_b_pw2, norm_final):
    bp, sp, d = x_prompt.shape
    bs, ss, _ = x_sample.shape
    n_first = bp * sp
    depth = norm_ffn1.shape[0]
    seq = dict(n_first=n_first, s_first=sp, s_second=ss)
    w1 = [w.astype(BF16) for w in (ffn1_w_gate, ffn1_w_up, ffn1_w_down)]
    w2 = [w.astype(BF16) for w in (ffn2_w_gate, ffn2_w_up, ffn2_w_down)]
    xs = [x_prompt.reshape(n_first, d), x_sample.reshape(bs * ss, d)]
    for i in range(depth):
        x = _ffn(xs if i == 0 else [x], norm_ffn1[i], *w1, i)
        kind, j = i % 3, i // 3
        if kind == 0:
            x = _gmlp(x, norm_mix[i], a_w_in[j], a_g_v[j], a_w_spatial[j], a_b_spatial[j], a_w_out[j])
        elif kind == 1:
            x = _dilated_attention(x, norm_mix[i], b_w_qkv[j], b_w_out[j], rel_bias, **seq)
        else:
            x = _conformer(x, norm_mix[i], c_w_pw1[j], c_b_pw1[j], c_w_dw[j], c_b_dw[j], c_g_norm[j],
                           c_w_pw2[j], c_b_pw2[j], **seq)
        if i < depth - 1:
            x = _ffn([x], norm_ffn2[i], *w2, i)
        else:
            y_prompt = _ffn([x], norm_ffn2[i], *w2, i, norm_final, row_range=(0, n_first))
            y_sample = _ffn([x], norm_ffn2[i], *w2, i, norm_final, row_range=(n_first, bs * ss))
    return y_prompt.reshape(bp, sp, d), y_sample.reshape(bs, ss, d)
```

```python
import functools
import math

import jax
import jax.numpy as jnp
from jax import lax
from jax.experimental import pallas as pl
from jax.experimental.pallas import tpu as pltpu

F32 = jnp.float32
BF16 = jnp.bfloat16

EPS = 1e-6
NEG_INF = -1e30

A_CHUNK = 128
A_GROUPS = 8
A_BLOCK_GROUPS = 2
B_PATTERNS = ((128, 1), (512, 4), (2048, 16))
B_HEADS_PER_GROUP = 6
B_HEAD_DIM = 64
B_GROUP_WIDTH = B_HEADS_PER_GROUP * B_HEAD_DIM
B_W_HALF = 64
B_QUERY_BLOCK = 128
B_MAX_TILE = 512
REL_BUCKETS = 32
REL_MAX_DIST = 1024
CONV_WIDTH = 31
CONV_PAD = CONV_WIDTH // 2
CONV_HALO = 16

V7X_VMEM_LIMIT_BYTES = 56 * 1024 * 1024
LANES = 128
SUBLANES = 8
CONV_ROWS = 256
FFN_SUBTILES = 4


def _params(n_grid_dims):
    return pltpu.CompilerParams(
        dimension_semantics=("arbitrary",) * n_grid_dims,
        vmem_limit_bytes=V7X_VMEM_LIMIT_BYTES,
    )


def _resident(shape):
    nd = len(shape)
    return pl.BlockSpec(shape, lambda *_: (0,) * nd, pipeline_mode=pl.Buffered(1))


def _rows(tm, width):
    return pl.BlockSpec((tm, width), lambda i: (i, 0))


def _rms(x, g):
    ms = jnp.mean(x * x, axis=-1, keepdims=True)
    return x * lax.rsqrt(ms + EPS) * g


def _dot(a, b):
    return jnp.dot(a, b, preferred_element_type=F32)


def _ffn_kernel(*refs, n_x, tiles_first, final_norm):
    x_refs, refs = refs[:n_x], refs[n_x:]
    g_ref, wg_ref, wu_ref, wd_ref = refs[:4]
    gf_ref = refs[4] if final_norm else None
    o_ref = refs[-1]
    sub = x_refs[0].shape[0] // FFN_SUBTILES
    for s in range(FFN_SUBTILES):
        rows = slice(s * sub, (s + 1) * sub)
        x = x_refs[0][rows, :]
        if n_x == 2:
            x = jnp.where(pl.program_id(0) < tiles_first, x, x_refs[1][rows, :])
        h = _rms(x, g_ref[...]).astype(BF16)
        gate = _dot(h, wg_ref[...])
        up = _dot(h, wu_ref[...])
        act = (gate * jax.nn.sigmoid(gate) * up).astype(BF16)
        out = x + 0.5 * _dot(act, wd_ref[...])
        if final_norm:
            out = _rms(out, gf_ref[...])
        o_ref[rows, :] = out


def _layer_weight(shape, layer):
    return pl.BlockSpec((None,) + tuple(shape[1:]), lambda i: (layer,) + (0,) * (len(shape) - 1),
                        pipeline_mode=pl.Buffered(1))


def _ffn(xs, g, wg, wu, wd, layer, g_final=None, *, row_range=None, tm=1024):
    d = xs[0].shape[1]
    n_rows = [x.shape[0] for x in xs]
    assert all(r % tm == 0 for r in n_rows)
    tiles_first = n_rows[0] // tm
    if row_range is not None:
        start, n = row_range
        assert len(xs) == 1 and start % tm == 0 and n % tm == 0
        x_specs = [pl.BlockSpec((tm, d), lambda i: (i + start // tm, 0))]
    elif len(xs) == 1:
        n = n_rows[0]
        x_specs = [_rows(tm, d)]
    else:
        n = sum(n_rows)
        x_specs = [pl.BlockSpec((tm, d), lambda i: (jnp.minimum(i, tiles_first - 1), 0)),
                   pl.BlockSpec((tm, d), lambda i: (jnp.maximum(i - tiles_first, 0), 0))]
    ins = list(xs) + [g.reshape(1, d), wg, wu, wd]
    specs = x_specs + [_resident((1, d)), _layer_weight(wg.shape, layer), _layer_weight(wu.shape, layer),
                       _layer_weight(wd.shape, layer)]
    if g_final is not None:
        ins.append(g_final.reshape(1, d))
        specs.append(_resident((1, d)))
    kern = functools.partial(_ffn_kernel, n_x=len(xs), tiles_first=tiles_first, final_norm=g_final is not None)
    return pl.pallas_call(
        kern,
        grid=(n // tm,),
        in_specs=specs,
        out_specs=_rows(tm, d),
        out_shape=jax.ShapeDtypeStruct((n, d), F32),
        compiler_params=_params(1),
        name="ffn",
    )(*ins)


def _gmlp_kernel(x_ref, g_ref, win_ref, gv_ref, ws_ref, bs_ref, wout_ref, o_ref, v_ref, uv_ref, *, tm):
    half = gv_ref.shape[1]
    gdim = half // A_GROUPS
    width = A_BLOCK_GROUPS * gdim
    x = x_ref[...]
    h = _rms(x, g_ref[...]).astype(BF16)
    ssq = jnp.zeros((tm, 1), F32)
    for blk in range(half // width):
        cols = slice(blk * width, (blk + 1) * width)
        v = jax.nn.gelu(_dot(h, win_ref[:, half + blk * width:half + (blk + 1) * width]))
        ssq = ssq + jnp.sum(v * v, axis=-1, keepdims=True)
        v_ref[:, cols] = v
    inv = lax.rsqrt(ssq / half + EPS)
    acc = x
    for blk in range(half // width):
        cols = slice(blk * width, (blk + 1) * width)
        u = jax.nn.gelu(_dot(h, win_ref[:, cols]))
        v = (v_ref[:, cols] * inv * gv_ref[:, cols]).astype(BF16)
        for c in range(tm // A_CHUNK):
            rows = slice(c * A_CHUNK, (c + 1) * A_CHUNK)
            for g in range(A_BLOCK_GROUPS):
                gc = slice(g * gdim, (g + 1) * gdim)
                sv = _dot(ws_ref[blk * A_BLOCK_GROUPS + g], v[rows, gc]) + bs_ref[blk * A_BLOCK_GROUPS + g]
                uv_ref[blk, rows, gc] = (u[rows, gc] * sv).astype(BF16)
        acc = acc + _dot(uv_ref[blk], wout_ref[cols, :])
    o_ref[...] = acc


def _gmlp(x, g, w_in, g_v, w_sp, b_sp, w_out, *, tm=512):
    n, d = x.shape
    hid = w_in.shape[1]
    half = hid // 2
    assert n % tm == 0 and tm % A_CHUNK == 0
    return pl.pallas_call(
        functools.partial(_gmlp_kernel, tm=tm),
        grid=(n // tm,),
        in_specs=[
            _rows(tm, d), _resident((1, d)), _resident((d, hid)), _resident((1, half)),
            _resident((A_GROUPS, A_CHUNK, A_CHUNK)), _resident((A_GROUPS, A_CHUNK, 1)),
            _resident((half, d)),
        ],
        out_specs=_rows(tm, d),
        out_shape=jax.ShapeDtypeStruct((n, d), F32),
        scratch_shapes=[pltpu.VMEM((tm, half), F32),
                        pltpu.VMEM((A_GROUPS // A_BLOCK_GROUPS, tm, A_BLOCK_GROUPS * (half // A_GROUPS)), BF16)],
        compiler_params=_params(1),
        name="gmlp",
    )(x, g.reshape(1, d), w_in.astype(BF16), g_v.reshape(1, half), w_sp.astype(BF16),
      b_sp.reshape(A_GROUPS, A_CHUNK, 1), w_out.astype(BF16))


def _seq_pos(row0, n_first, s_first, s_second):
    in_first = row0 < n_first
    pos = jnp.where(in_first, row0 % s_first, (row0 - n_first) % s_second)
    return pos, jnp.where(in_first, s_first, s_second)


def _conv_kernel(xp_ref, xc_ref, xn_ref, g_ref, w1_ref, b1_ref, wdw_ref, bdw_ref, gn_ref, w2_ref, b2_ref,
                 o_ref, glu_ref, conv_ref, shift_ref, *, t, n_first, s_first, s_second):
    d = xc_ref.shape[1]
    ext = t + 2 * CONV_HALO
    pos0, seq_len = _seq_pos(pl.program_id(0) * t, n_first, s_first, s_second)
    xc = xc_ref[...]
    xa = jnp.concatenate([xp_ref[...], xc, xn_ref[...]], axis=0)
    h = _rms(xa, g_ref[...]).astype(BF16)
    p = _dot(h, w1_ref[...]) + b1_ref[...]
    glu = p[:, :d] * jax.nn.sigmoid(p[:, d:])
    pos = pos0 - CONV_HALO + lax.broadcasted_iota(jnp.int32, (ext, 1), 0)
    glu = jnp.where((pos >= 0) & (pos < seq_len), glu, 0.0)
    n_cb = d // LANES
    for cb in range(n_cb):
        glu_ref[cb] = glu[:, cb * LANES:(cb + 1) * LANES]
    base = CONV_HALO - CONV_PAD
    span = ((CONV_WIDTH - 1) // SUBLANES) * SUBLANES

    def lane_block(cb, carry):
        for b in range(SUBLANES):
            shift_ref[b] = glu_ref[cb, base + b:base + b + t + span, :]
        for rb in range(t // CONV_ROWS):
            acc = jnp.zeros((CONV_ROWS, LANES), F32)
            for j in range(CONV_WIDTH):
                r0 = rb * CONV_ROWS + j - j % SUBLANES
                acc = acc + wdw_ref[cb, j:j + 1, :] * shift_ref[j % SUBLANES, r0:r0 + CONV_ROWS, :]
            conv_ref[cb, rb * CONV_ROWS:(rb + 1) * CONV_ROWS, :] = acc
        return carry

    lax.fori_loop(0, n_cb, lane_block, 0)
    c = jnp.concatenate([conv_ref[cb] for cb in range(n_cb)], axis=1) + bdw_ref[...]
    c = _rms(c, gn_ref[...])
    c = (c * jax.nn.sigmoid(c)).astype(BF16)
    o_ref[...] = xc + _dot(c, w2_ref[...]) + b2_ref[...]


def _conformer(x, g, w1, b1, wdw, bdw, gn, w2, b2, *, n_first, s_first, s_second, t=512):
    n, d = x.shape
    assert n % t == 0 and s_first % t == 0 and s_second % t == 0 and t % CONV_HALO == 0
    hb = t // CONV_HALO
    n_hb = n // CONV_HALO
    n_cb = d // LANES
    kern = functools.partial(_conv_kernel, t=t, n_first=n_first, s_first=s_first, s_second=s_second)
    return pl.pallas_call(
        kern,
        grid=(n // t,),
        in_specs=[
            pl.BlockSpec((CONV_HALO, d), lambda i: (jnp.maximum(i * hb - 1, 0), 0)),
            _rows(t, d),
            pl.BlockSpec((CONV_HALO, d), lambda i: (jnp.minimum((i + 1) * hb, n_hb - 1), 0)),
            _resident((1, d)), _resident((d, 2 * d)), _resident((1, 2 * d)),
            _resident((n_cb, CONV_WIDTH, LANES)), _resident((1, d)), _resident((1, d)),
            _resident((d, d)), _resident((1, d)),
        ],
        out_specs=_rows(t, d),
        out_shape=jax.ShapeDtypeStruct((n, d), F32),
        scratch_shapes=[pltpu.VMEM((n_cb, t + 2 * CONV_HALO, LANES), F32),
                        pltpu.VMEM((n_cb, t, LANES), F32),
                        pltpu.VMEM((SUBLANES, t + (CONV_WIDTH - 1) // SUBLANES * SUBLANES, LANES), F32)],
        compiler_params=_params(1),
        name="conformer",
    )(x, x, x, g.reshape(1, d), w1.astype(BF16), b1.reshape(1, 2 * d),
      wdw.reshape(CONV_WIDTH, d // LANES, LANES).transpose(1, 0, 2), bdw.reshape(1, d),
      gn.reshape(1, d), w2.astype(BF16), b2.reshape(1, d))


def _qkv_kernel(x_ref, g_ref, w_ref, *rest, tm):
    o_refs, h_ref = rest[:-1], rest[-1]
    hn = _rms(x_ref[...], g_ref[...])
    n_cb = h_ref.shape[0]
    for cb in range(n_cb):
        h_ref[cb] = hn[:, cb * LANES:(cb + 1) * LANES]
    gw = B_GROUP_WIDTH
    scale = 1.0 / math.sqrt(B_HEAD_DIM)
    for grp, (o_ref, (_, dil)) in enumerate(zip(o_refs, B_PATTERNS)):
        rows = tm // dil
        if dil == 1:
            h = hn
        else:
            h = jnp.concatenate(
                [jnp.concatenate([h_ref[cb, pl.ds(r, rows, stride=dil), :] for cb in range(n_cb)], axis=1)
                 for r in range(dil)], axis=0)
        qkv = _dot(h.astype(BF16), w_ref[grp])
        for r in range(dil):
            blk = qkv[r * rows:(r + 1) * rows]
            o_ref[r, :, :gw] = (blk[:, :gw] * scale).astype(BF16)
            o_ref[r, :, gw:] = blk[:, gw:].astype(BF16)


def _qkv_proj(x, g, w, *, tm=512):
    n, d = x.shape
    ng = len(B_PATTERNS)
    gw = B_GROUP_WIDTH
    assert n % tm == 0 and w.shape[1] == 3 * ng * gw
    w_grp = w.astype(BF16).reshape(d, 3, ng, gw).transpose(2, 0, 1, 3).reshape(ng, d, 3 * gw)
    return pl.pallas_call(
        functools.partial(_qkv_kernel, tm=tm),
        grid=(n // tm,),
        in_specs=[_rows(tm, d), _resident((1, d)), _resident((ng, d, 3 * gw))],
        out_specs=[pl.BlockSpec((dil, tm // dil, 3 * gw), lambda i: (0, i, 0)) for _, dil in B_PATTERNS],
        out_shape=[jax.ShapeDtypeStruct((dil, n // dil, 3 * gw), BF16) for _, dil in B_PATTERNS],
        scratch_shapes=[pltpu.VMEM((d // LANES, tm, LANES), F32)],
        compiler_params=_params(1),
        name="qkv_proj",
    )(x, g.reshape(1, d), w_grp)


def _t5_bucket(rel):
    half = REL_BUCKETS // 2
    max_exact = half // 2
    ret = jnp.where(rel > 0, half, 0)
    n = jnp.abs(rel)
    nf = jnp.maximum(n, 1).astype(jnp.float32)
    large = max_exact + (jnp.log(nf / max_exact) / math.log(REL_MAX_DIST / max_exact)
                         * (half - max_exact)).astype(jnp.int32)
    large = jnp.minimum(large, half - 1)
    return ret + jnp.where(n < max_exact, n, large)


def _bias_kernel(table_ref, bucket_ref, rel_ref, o_ref):
    bucket = bucket_ref[...]
    tk = bucket.shape[1]
    in_band = jnp.abs(rel_ref[...]) <= B_W_HALF
    kk = lax.broadcasted_iota(jnp.int32, bucket.shape, 1)
    after_start, before_end = kk >= B_W_HALF, kk < tk - B_W_HALF
    masks = [in_band, in_band & after_start, in_band & before_end, in_band & after_start & before_end]
    for h in range(o_ref.shape[1]):
        acc = jnp.zeros(bucket.shape, F32)
        for b in range(REL_BUCKETS):
            acc = jnp.where(bucket == b, table_ref[b, h], acc)
        for e, mask in enumerate(masks):
            o_ref[e, h] = jnp.where(mask, acc, NEG_INF)


def _band_bias(table, dil):
    t = B_QUERY_BLOCK
    tk = t + 2 * B_W_HALF
    q_idx = jnp.arange(t, dtype=jnp.int32)
    k_idx = jnp.arange(tk, dtype=jnp.int32)
    rel = k_idx[None, :] - B_W_HALF - q_idx[:, None]
    bucket = _t5_bucket(rel * dil)
    nh = table.shape[1]
    return pl.pallas_call(
        _bias_kernel,
        in_specs=[pl.BlockSpec(memory_space=pltpu.SMEM), pl.BlockSpec((t, tk), lambda: (0, 0)),
                  pl.BlockSpec((t, tk), lambda: (0, 0))],
        out_specs=pl.BlockSpec((4, nh, t, tk), lambda: (0, 0, 0, 0)),
        out_shape=jax.ShapeDtypeStruct((4, nh, t, tk), F32),
        name="band_bias",
    )(table, bucket, rel)


def _attn_kernel(q_ref, kp_ref, kc_ref, kn_ref, vp_ref, vc_ref, vn_ref, bias_ref, o_ref, lse_ref,
                 *, t, rows_first, l_first, l_second):
    qb = B_QUERY_BLOCK
    n_qb = t // qb
    pos0, seq_len = _seq_pos(pl.program_id(1) * t, rows_first, l_first, l_second)
    at_start = (pos0 == 0).astype(jnp.int32)
    at_end = (pos0 + t == seq_len).astype(jnp.int32)
    lane = lax.broadcasted_iota(jnp.int32, (1, LANES), 1)
    low = lane < B_HEAD_DIM
    for res in range(q_ref.shape[0]):
        k = jnp.concatenate([kp_ref[res], kc_ref[res], kn_ref[res]], axis=0)
        v = jnp.concatenate([vp_ref[res], vc_ref[res], vn_ref[res]], axis=0)
        q = q_ref[res]
        for j in range(n_qb):
            edge = (at_start if j == 0 else 0) + (2 * at_end if j == n_qb - 1 else 0)
            q_rows = slice(j * qb, (j + 1) * qb)
            k_rows = slice(j * qb, (j + 1) * qb + 2 * B_W_HALF)
            for pair in range(B_GROUP_WIDTH // LANES):
                cols = slice(pair * LANES, (pair + 1) * LANES)
                qp, kp, vp = q[q_rows, cols], k[k_rows, cols], v[k_rows, cols]
                outs, lses = [], []
                for sub in range(2):
                    qh = jnp.where(low if sub == 0 else ~low, qp, jnp.zeros_like(qp))
                    s = lax.dot_general(qh, kp, (((1,), (1,)), ((), ())), preferred_element_type=F32)
                    s = s + bias_ref[edge, 2 * pair + sub]
                    m = jnp.max(s, axis=-1, keepdims=True)
                    p = jnp.exp(s - m)
                    l = jnp.sum(p, axis=-1, keepdims=True)
                    outs.append(_dot(p.astype(BF16), vp) * (1.0 / l))
                    lses.append(m + jnp.log(l))
                o_ref[res, q_rows, cols] = jnp.where(low, outs[0], outs[1])
                lse_ref[res, q_rows, cols] = jnp.where(low, lses[0], lses[1])


def _band_attention(qkv, bias, *, n_first, s_first, s_second):
    dil, rows, _ = qkv.shape
    gw = B_GROUP_WIDTH
    l_first, l_second = s_first // dil, s_second // dil
    t = min(B_MAX_TILE, l_first)
    assert l_first % t == 0 and l_second % t == 0 and t % B_QUERY_BLOCK == 0
    n_res = min(dil, B_MAX_TILE // t)
    assert dil % n_res == 0
    hb = t // B_W_HALF
    n_hb = rows // B_W_HALF

    def cur(which):
        return pl.BlockSpec((n_res, t, gw), lambda r, i: (r, i, which))

    def prev(which):
        return pl.BlockSpec((n_res, B_W_HALF, gw), lambda r, i: (r, jnp.maximum(i * hb - 1, 0), which))

    def nxt(which):
        return pl.BlockSpec((n_res, B_W_HALF, gw), lambda r, i: (r, jnp.minimum((i + 1) * hb, n_hb - 1), which))

    kern = functools.partial(_attn_kernel, t=t, rows_first=n_first // dil, l_first=l_first, l_second=l_second)
    out_spec = pl.BlockSpec((n_res, t, gw), lambda r, i: (r, i, 0))
    return pl.pallas_call(
        kern,
        grid=(dil // n_res, rows // t),
        in_specs=[cur(0), prev(1), cur(1), nxt(1), prev(2), cur(2), nxt(2),
                  pl.BlockSpec(bias.shape, lambda r, i: (0, 0, 0, 0), pipeline_mode=pl.Buffered(1))],
        out_specs=[out_spec, out_spec],
        out_shape=[jax.ShapeDtypeStruct((dil, rows, gw), F32)] * 2,
        compiler_params=_params(2),
        name=f"band_attn_d{dil}",
    )(qkv, qkv, qkv, qkv, qkv, qkv, qkv, bias)


def _combine_kernel(x_ref, o0_ref, o1_ref, o2_ref, l0_ref, l1_ref, l2_ref, w_ref, out_ref, o_scr, l_scr):
    n_cb = o_scr.shape[1]
    for grp, (o_ref, l_ref) in enumerate(((o0_ref, l0_ref), (o1_ref, l1_ref), (o2_ref, l2_ref))):
        dil, rows, _ = o_ref.shape
        for r in range(dil):
            for cb in range(n_cb):
                cols = slice(cb * LANES, (cb + 1) * LANES)
                o_scr[grp, cb, pl.ds(r, rows, stride=dil), :] = o_ref[r, :, cols]
                l_scr[grp, cb, pl.ds(r, rows, stride=dil), :] = l_ref[r, :, cols]

    def full(scr, grp):
        return jnp.concatenate([scr[grp, cb] for cb in range(n_cb)], axis=1)

    l0, l1, l2 = full(l_scr, 0), full(l_scr, 1), full(l_scr, 2)
    m = jnp.maximum(jnp.maximum(l0, l1), l2)
    e0, e1, e2 = jnp.exp(l0 - m), jnp.exp(l1 - m), jnp.exp(l2 - m)
    inv = 1.0 / (e0 + e1 + e2)
    acc = x_ref[...]
    for grp, e in enumerate((e0, e1, e2)):
        acc = acc + _dot((full(o_scr, grp) * (e * inv)).astype(BF16), w_ref[grp])
    out_ref[...] = acc


def _combine_proj(x, outs, lses, w_out, *, tm=512):
    n, d = x.shape
    gw = B_GROUP_WIDTH
    ng = len(outs)
    assert n % tm == 0
    grouped = [pl.BlockSpec((o.shape[0], tm // o.shape[0], gw), lambda i: (0, i, 0)) for o in outs]
    return pl.pallas_call(
        _combine_kernel,
        grid=(n // tm,),
        in_specs=[_rows(tm, d)] + grouped + grouped + [_resident((ng, gw, d))],
        out_specs=_rows(tm, d),
        out_shape=jax.ShapeDtypeStruct((n, d), F32),
        scratch_shapes=[pltpu.VMEM((ng, gw // LANES, tm, LANES), F32)] * 2,
        compiler_params=_params(1),
        name="attn_combine",
    )(x, *outs, *lses, w_out.astype(BF16).reshape(ng, gw, d))


def _dilated_attention(x, g, w_qkv, w_out, rel_bias, *, n_first, s_first, s_second):
    qkvs = _qkv_proj(x, g, w_qkv)
    outs, lses = [], []
    for grp, (window, dil) in enumerate(B_PATTERNS):
        assert window // (2 * dil) == B_W_HALF
        table = rel_bias[:, grp * B_HEADS_PER_GROUP:(grp + 1) * B_HEADS_PER_GROUP]
        bias = _band_bias(table, dil)
        o, l = _band_attention(qkvs[grp], bias, n_first=n_first, s_first=s_first, s_second=s_second)
        outs.append(o)
        lses.append(l)
    return _combine_proj(x, outs, lses, w_out)


def kernel(x_prompt, x_sample, norm_ffn1, ffn1_w_gate, ffn1_w_up, ffn1_w_down, norm_mix, norm_ffn2, ffn2_w_gate, ffn2_w_up, ffn2_w_down, a_w_in, a_g_v, a_w_spatial, a_b_spatial, a_w_out, b_w_qkv, b_w_out, rel_bias, c_w_pw1, c_b_pw1, c_w_dw, c_b_dw, c_g_norm, c_w_pw2, c_b_pw2, norm_final):
    bp, sp, d = x_prompt.shape
    bs, ss, _ = x_sample.shape
    n_first = bp * sp
    depth = norm_ffn1.shape[0]
    seq = dict(n_first=n_first, s_first=sp, s_second=ss)
    w1 = [w.astype(BF16) for w in (ffn1_w_gate, ffn1_w_up, ffn1_w_down)]
    w2 = [w.astype(BF16) for w in (ffn2_w_gate, ffn2_w_up, ffn2_w_down)]
    xs = [x_prompt.reshape(n_first, d), x_sample.reshape(bs * ss, d)]
    for i in range(depth):
        x = _ffn(xs if i == 0 else [x], norm_ffn1[i], *w1, i)
        kind, j = i % 3, i // 3
        if kind == 0:
            x = _gmlp(x, norm_mix[i], a_w_in[j], a_g_v[j], a_w_spatial[j], a_b_spatial[j], a_w_out[j])
        elif kind == 1:
            x = _dilated_attention(x, norm_mix[i], b_w_qkv[j], b_w_out[j], rel_bias, **seq)
        else:
            x = _conformer(x, norm_mix[i], c_w_pw1[j], c_b_pw1[j], c_w_dw[j], c_b_dw[j], c_g_norm[j],
                           c_w_pw2[j], c_b_pw2[j], **seq)
        if i < depth - 1:
            x = _ffn([x], norm_ffn2[i], *w2, i)
        else:
            y_prompt = _ffn([x], norm_ffn2[i], *w2, i, norm_final, row_range=(0, n_first))
            y_sample = _ffn([x], norm_ffn2[i], *w2, i, norm_final, row_range=(n_first, bs * ss))
    return y_prompt.reshape(bp, sp, d), y_sample.reshape(bs, ss, d)
```

```python
import functools
import math

import jax
import jax.numpy as jnp
from jax import lax
from jax.experimental import pallas as pl
from jax.experimental.pallas import tpu as pltpu

F32 = jnp.float32
BF16 = jnp.bfloat16

EPS = 1e-6
NEG_INF = -1e30

A_CHUNK = 128
A_GROUPS = 8
A_BLOCK_GROUPS = 2
B_PATTERNS = ((128, 1), (512, 4), (2048, 16))
B_HEADS_PER_GROUP = 6
B_HEAD_DIM = 64
B_GROUP_WIDTH = B_HEADS_PER_GROUP * B_HEAD_DIM
B_W_HALF = 64
B_QUERY_BLOCK = 128
B_MAX_TILE = 512
REL_BUCKETS = 32
REL_MAX_DIST = 1024
CONV_WIDTH = 31
CONV_PAD = CONV_WIDTH // 2
CONV_HALO = 16

V7X_VMEM_LIMIT_BYTES = 56 * 1024 * 1024
LANES = 128
SUBLANES = 8
CONV_ROWS = 256
FFN_SUBTILES = 4
COMBINE_FFN_SUBTILES = 2


def _params(n_grid_dims):
    return pltpu.CompilerParams(
        dimension_semantics=("arbitrary",) * n_grid_dims,
        vmem_limit_bytes=V7X_VMEM_LIMIT_BYTES,
    )


def _resident(shape):
    nd = len(shape)
    return pl.BlockSpec(shape, lambda *_: (0,) * nd, pipeline_mode=pl.Buffered(1))


def _rows(tm, width):
    return pl.BlockSpec((tm, width), lambda i: (i, 0))


def _rms(x, g):
    ms = jnp.mean(x * x, axis=-1, keepdims=True)
    return x * lax.rsqrt(ms + EPS) * g


def _dot(a, b):
    return jnp.dot(a, b, preferred_element_type=F32)


def _swiglu_residual(x, g_ref, wg_ref, wu_ref, wd_ref):
    h = _rms(x, g_ref[...]).astype(BF16)
    gate = _dot(h, wg_ref[...])
    up = _dot(h, wu_ref[...])
    act = (gate * jax.nn.sigmoid(gate) * up).astype(BF16)
    return x + 0.5 * _dot(act, wd_ref[...])


def _ffn_kernel(*refs, n_x, tiles_first, final_norm):
    x_refs, refs = refs[:n_x], refs[n_x:]
    g_ref, wg_ref, wu_ref, wd_ref = refs[:4]
    gf_ref = refs[4] if final_norm else None
    o_ref = refs[-1]
    sub = x_refs[0].shape[0] // FFN_SUBTILES
    for s in range(FFN_SUBTILES):
        rows = slice(s * sub, (s + 1) * sub)
        x = x_refs[0][rows, :]
        if n_x == 2:
            x = jnp.where(pl.program_id(0) < tiles_first, x, x_refs[1][rows, :])
        out = _swiglu_residual(x, g_ref, wg_ref, wu_ref, wd_ref)
        if final_norm:
            out = _rms(out, gf_ref[...])
        o_ref[rows, :] = out


def _layer_weight(shape, layer):
    return pl.BlockSpec((None,) + tuple(shape[1:]), lambda i: (layer,) + (0,) * (len(shape) - 1),
                        pipeline_mode=pl.Buffered(1))


def _ffn(xs, g, wg, wu, wd, layer, g_final=None, *, row_range=None, tm=1024):
    d = xs[0].shape[1]
    n_rows = [x.shape[0] for x in xs]
    assert all(r % tm == 0 for r in n_rows)
    tiles_first = n_rows[0] // tm
    if row_range is not None:
        start, n = row_range
        assert len(xs) == 1 and start % tm == 0 and n % tm == 0
        x_specs = [pl.BlockSpec((tm, d), lambda i: (i + start // tm, 0))]
    elif len(xs) == 1:
        n = n_rows[0]
        x_specs = [_rows(tm, d)]
    else:
        n = sum(n_rows)
        x_specs = [pl.BlockSpec((tm, d), lambda i: (jnp.minimum(i, tiles_first - 1), 0)),
                   pl.BlockSpec((tm, d), lambda i: (jnp.maximum(i - tiles_first, 0), 0))]
    ins = list(xs) + [g.reshape(1, d), wg, wu, wd]
    specs = x_specs + [_resident((1, d)), _layer_weight(wg.shape, layer), _layer_weight(wu.shape, layer),
                       _layer_weight(wd.shape, layer)]
    if g_final is not None:
        ins.append(g_final.reshape(1, d))
        specs.append(_resident((1, d)))
    kern = functools.partial(_ffn_kernel, n_x=len(xs), tiles_first=tiles_first, final_norm=g_final is not None)
    return pl.pallas_call(
        kern,
        grid=(n // tm,),
        in_specs=specs,
        out_specs=_rows(tm, d),
        out_shape=jax.ShapeDtypeStruct((n, d), F32),
        compiler_params=_params(1),
        name="ffn",
    )(*ins)


def _gmlp_kernel(x_ref, g_ref, win_ref, gv_ref, ws_ref, bs_ref, wout_ref, o_ref, v_ref, uv_ref, *, tm):
    half = gv_ref.shape[1]
    gdim = half // A_GROUPS
    width = A_BLOCK_GROUPS * gdim
    x = x_ref[...]
    h = _rms(x, g_ref[...]).astype(BF16)
    ssq = jnp.zeros((tm, 1), F32)
    for blk in range(half // width):
        cols = slice(blk * width, (blk + 1) * width)
        v = jax.nn.gelu(_dot(h, win_ref[:, half + blk * width:half + (blk + 1) * width]))
        ssq = ssq + jnp.sum(v * v, axis=-1, keepdims=True)
        v_ref[:, cols] = v
    inv = lax.rsqrt(ssq / half + EPS)
    acc = x
    for blk in range(half // width):
        cols = slice(blk * width, (blk + 1) * width)
        u = jax.nn.gelu(_dot(h, win_ref[:, cols]))
        v = (v_ref[:, cols] * inv * gv_ref[:, cols]).astype(BF16)
        for c in range(tm // A_CHUNK):
            rows = slice(c * A_CHUNK, (c + 1) * A_CHUNK)
            for g in range(A_BLOCK_GROUPS):
                gc = slice(g * gdim, (g + 1) * gdim)
                sv = _dot(ws_ref[blk * A_BLOCK_GROUPS + g], v[rows, gc]) + bs_ref[blk * A_BLOCK_GROUPS + g]
                uv_ref[blk, rows, gc] = (u[rows, gc] * sv).astype(BF16)
        acc = acc + _dot(uv_ref[blk], wout_ref[cols, :])
    o_ref[...] = acc


def _gmlp(x, g, w_in, g_v, w_sp, b_sp, w_out, *, tm=512):
    n, d = x.shape
    hid = w_in.shape[1]
    half = hid // 2
    assert n % tm == 0 and tm % A_CHUNK == 0
    return pl.pallas_call(
        functools.partial(_gmlp_kernel, tm=tm),
        grid=(n // tm,),
        in_specs=[
            _rows(tm, d), _resident((1, d)), _resident((d, hid)), _resident((1, half)),
            _resident((A_GROUPS, A_CHUNK, A_CHUNK)), _resident((A_GROUPS, A_CHUNK, 1)),
            _resident((half, d)),
        ],
        out_specs=_rows(tm, d),
        out_shape=jax.ShapeDtypeStruct((n, d), F32),
        scratch_shapes=[pltpu.VMEM((tm, half), F32),
                        pltpu.VMEM((A_GROUPS // A_BLOCK_GROUPS, tm, A_BLOCK_GROUPS * (half // A_GROUPS)), BF16)],
        compiler_params=_params(1),
        name="gmlp",
    )(x, g.reshape(1, d), w_in.astype(BF16), g_v.reshape(1, half), w_sp.astype(BF16),
      b_sp.reshape(A_GROUPS, A_CHUNK, 1), w_out.astype(BF16))


def _seq_pos(row0, n_first, s_first, s_second):
    in_first = row0 < n_first
    pos = jnp.where(in_first, row0 % s_first, (row0 - n_first) % s_second)
    return pos, jnp.where(in_first, s_first, s_second)


def _conv_kernel(xp_ref, xc_ref, xn_ref, g_ref, w1_ref, b1_ref, wdw_ref, bdw_ref, gn_ref, w2_ref, b2_ref,
                 o_ref, glu_ref, conv_ref, shift_ref, *, t, n_first, s_first, s_second):
    d = xc_ref.shape[1]
    ext = t + 2 * CONV_HALO
    pos0, seq_len = _seq_pos(pl.program_id(0) * t, n_first, s_first, s_second)
    xc = xc_ref[...]
    xa = jnp.concatenate([xp_ref[...], xc, xn_ref[...]], axis=0)
    h = _rms(xa, g_ref[...]).astype(BF16)
    p = _dot(h, w1_ref[...]) + b1_ref[...]
    glu = p[:, :d] * jax.nn.sigmoid(p[:, d:])
    pos = pos0 - CONV_HALO + lax.broadcasted_iota(jnp.int32, (ext, 1), 0)
    glu = jnp.where((pos >= 0) & (pos < seq_len), glu, 0.0)
    n_cb = d // LANES
    for cb in range(n_cb):
        glu_ref[cb] = glu[:, cb * LANES:(cb + 1) * LANES]
    base = CONV_HALO - CONV_PAD
    span = ((CONV_WIDTH - 1) // SUBLANES) * SUBLANES

    def lane_block(cb, carry):
        for b in range(SUBLANES):
            shift_ref[b] = glu_ref[cb, base + b:base + b + t + span, :]
        for rb in range(t // CONV_ROWS):
            acc = jnp.zeros((CONV_ROWS, LANES), F32)
            for j in range(CONV_WIDTH):
                r0 = rb * CONV_ROWS + j - j % SUBLANES
                acc = acc + wdw_ref[cb, j:j + 1, :] * shift_ref[j % SUBLANES, r0:r0 + CONV_ROWS, :]
            conv_ref[cb, rb * CONV_ROWS:(rb + 1) * CONV_ROWS, :] = acc
        return carry

    lax.fori_loop(0, n_cb, lane_block, 0)
    c = jnp.concatenate([conv_ref[cb] for cb in range(n_cb)], axis=1) + bdw_ref[...]
    c = _rms(c, gn_ref[...])
    c = (c * jax.nn.sigmoid(c)).astype(BF16)
    o_ref[...] = xc + _dot(c, w2_ref[...]) + b2_ref[...]


def _conformer(x, g, w1, b1, wdw, bdw, gn, w2, b2, *, n_first, s_first, s_second, t=512):
    n, d = x.shape
    assert n % t == 0 and s_first % t == 0 and s_second % t == 0 and t % CONV_HALO == 0
    hb = t // CONV_HALO
    n_hb = n // CONV_HALO
    n_cb = d // LANES
    kern = functools.partial(_conv_kernel, t=t, n_first=n_first, s_first=s_first, s_second=s_second)
    return pl.pallas_call(
        kern,
        grid=(n // t,),
        in_specs=[
            pl.BlockSpec((CONV_HALO, d), lambda i: (jnp.maximum(i * hb - 1, 0), 0)),
            _rows(t, d),
            pl.BlockSpec((CONV_HALO, d), lambda i: (jnp.minimum((i + 1) * hb, n_hb - 1), 0)),
            _resident((1, d)), _resident((d, 2 * d)), _resident((1, 2 * d)),
            _resident((n_cb, CONV_WIDTH, LANES)), _resident((1, d)), _resident((1, d)),
            _resident((d, d)), _resident((1, d)),
        ],
        out_specs=_rows(t, d),
        out_shape=jax.ShapeDtypeStruct((n, d), F32),
        scratch_shapes=[pltpu.VMEM((n_cb, t + 2 * CONV_HALO, LANES), F32),
                        pltpu.VMEM((n_cb, t, LANES), F32),
                        pltpu.VMEM((SUBLANES, t + (CONV_WIDTH - 1) // SUBLANES * SUBLANES, LANES), F32)],
        compiler_params=_params(1),
        name="conformer",
    )(x, x, x, g.reshape(1, d), w1.astype(BF16), b1.reshape(1, 2 * d),
      wdw.reshape(CONV_WIDTH, d // LANES, LANES).transpose(1, 0, 2), bdw.reshape(1, d),
      gn.reshape(1, d), w2.astype(BF16), b2.reshape(1, d))


def _qkv_kernel(x_ref, g_ref, w_ref, *rest, tm):
    o_refs, h_ref = rest[:-1], rest[-1]
    hn = _rms(x_ref[...], g_ref[...])
    n_cb = h_ref.shape[0]
    for cb in range(n_cb):
        h_ref[cb] = hn[:, cb * LANES:(cb + 1) * LANES]
    gw = B_GROUP_WIDTH
    scale = 1.0 / math.sqrt(B_HEAD_DIM)
    for grp, (o_ref, (_, dil)) in enumerate(zip(o_refs, B_PATTERNS)):
        rows = tm // dil
        if dil == 1:
            h = hn
        else:
            h = jnp.concatenate(
                [jnp.concatenate([h_ref[cb, pl.ds(r, rows, stride=dil), :] for cb in range(n_cb)], axis=1)
                 for r in range(dil)], axis=0)
        qkv = _dot(h.astype(BF16), w_ref[grp])
        for r in range(dil):
            blk = qkv[r * rows:(r + 1) * rows]
            o_ref[r, :, :gw] = (blk[:, :gw] * scale).astype(BF16)
            o_ref[r, :, gw:] = blk[:, gw:].astype(BF16)


def _qkv_proj(x, g, w, *, tm=1024):
    n, d = x.shape
    ng = len(B_PATTERNS)
    gw = B_GROUP_WIDTH
    assert n % tm == 0 and w.shape[1] == 3 * ng * gw
    w_grp = w.astype(BF16).reshape(d, 3, ng, gw).transpose(2, 0, 1, 3).reshape(ng, d, 3 * gw)
    return pl.pallas_call(
        functools.partial(_qkv_kernel, tm=tm),
        grid=(n // tm,),
        in_specs=[_rows(tm, d), _resident((1, d)), _resident((ng, d, 3 * gw))],
        out_specs=[pl.BlockSpec((dil, tm // dil, 3 * gw), lambda i: (0, i, 0)) for _, dil in B_PATTERNS],
        out_shape=[jax.ShapeDtypeStruct((dil, n // dil, 3 * gw), BF16) for _, dil in B_PATTERNS],
        scratch_shapes=[pltpu.VMEM((d // LANES, tm, LANES), F32)],
        compiler_params=_params(1),
        name="qkv_proj",
    )(x, g.reshape(1, d), w_grp)


def _t5_bucket(rel):
    half = REL_BUCKETS // 2
    max_exact = half // 2
    ret = jnp.where(rel > 0, half, 0)
    n = jnp.abs(rel)
    nf = jnp.maximum(n, 1).astype(jnp.float32)
    large = max_exact + (jnp.log(nf / max_exact) / math.log(REL_MAX_DIST / max_exact)
                         * (half - max_exact)).astype(jnp.int32)
    large = jnp.minimum(large, half - 1)
    return ret + jnp.where(n < max_exact, n, large)


def _bias_kernel(table_ref, bucket_ref, rel_ref, o_ref):
    bucket = bucket_ref[...]
    tk = bucket.shape[1]
    in_band = jnp.abs(rel_ref[...]) <= B_W_HALF
    kk = lax.broadcasted_iota(jnp.int32, bucket.shape, 1)
    after_start, before_end = kk >= B_W_HALF, kk < tk - B_W_HALF
    masks = [in_band, in_band & after_start, in_band & before_end, in_band & after_start & before_end]
    for h in range(o_ref.shape[1]):
        acc = jnp.zeros(bucket.shape, F32)
        for b in range(REL_BUCKETS):
            acc = jnp.where(bucket == b, table_ref[b, h], acc)
        for e, mask in enumerate(masks):
            o_ref[e, h] = jnp.where(mask, acc, NEG_INF)


def _band_bias(table, dil):
    t = B_QUERY_BLOCK
    tk = t + 2 * B_W_HALF
    q_idx = jnp.arange(t, dtype=jnp.int32)
    k_idx = jnp.arange(tk, dtype=jnp.int32)
    rel = k_idx[None, :] - B_W_HALF - q_idx[:, None]
    bucket = _t5_bucket(rel * dil)
    nh = table.shape[1]
    return pl.pallas_call(
        _bias_kernel,
        in_specs=[pl.BlockSpec(memory_space=pltpu.SMEM), pl.BlockSpec((t, tk), lambda: (0, 0)),
                  pl.BlockSpec((t, tk), lambda: (0, 0))],
        out_specs=pl.BlockSpec((4, nh, t, tk), lambda: (0, 0, 0, 0)),
        out_shape=jax.ShapeDtypeStruct((4, nh, t, tk), F32),
        name="band_bias",
    )(table, bucket, rel)


def _attn_kernel(q_ref, kp_ref, kc_ref, kn_ref, vp_ref, vc_ref, vn_ref, bias_ref, o_ref, lse_ref,
                 *, t, rows_first, l_first, l_second):
    qb = B_QUERY_BLOCK
    n_qb = t // qb
    pos0, seq_len = _seq_pos(pl.program_id(1) * t, rows_first, l_first, l_second)
    at_start = (pos0 == 0).astype(jnp.int32)
    at_end = (pos0 + t == seq_len).astype(jnp.int32)
    lane = lax.broadcasted_iota(jnp.int32, (1, LANES), 1)
    low = lane < B_HEAD_DIM
    for res in range(q_ref.shape[0]):
        k = jnp.concatenate([kp_ref[res], kc_ref[res], kn_ref[res]], axis=0)
        v = jnp.concatenate([vp_ref[res], vc_ref[res], vn_ref[res]], axis=0)
        q = q_ref[res]
        for j in range(n_qb):
            edge = (at_start if j == 0 else 0) + (2 * at_end if j == n_qb - 1 else 0)
            q_rows = slice(j * qb, (j + 1) * qb)
            k_rows = slice(j * qb, (j + 1) * qb + 2 * B_W_HALF)
            for pair in range(B_GROUP_WIDTH // LANES):
                cols = slice(pair * LANES, (pair + 1) * LANES)
                qp, kp, vp = q[q_rows, cols], k[k_rows, cols], v[k_rows, cols]
                outs, lses = [], []
                for sub in range(2):
                    qh = jnp.where(low if sub == 0 else ~low, qp, jnp.zeros_like(qp))
                    s = lax.dot_general(qh, kp, (((1,), (1,)), ((), ())), preferred_element_type=F32)
                    s = s + bias_ref[edge, 2 * pair + sub]
                    m = jnp.max(s, axis=-1, keepdims=True)
                    p = jnp.exp(s - m)
                    l = jnp.sum(p, axis=-1, keepdims=True)
                    outs.append(_dot(p.astype(BF16), vp) * (1.0 / l))
                    lses.append(m + jnp.log(l))
                o_ref[res, q_rows, cols] = jnp.where(low, outs[0], outs[1])
                lse_ref[res, q_rows, cols] = jnp.where(low, lses[0], lses[1])


def _band_attention(qkv, bias, *, n_first, s_first, s_second):
    dil, rows, _ = qkv.shape
    gw = B_GROUP_WIDTH
    l_first, l_second = s_first // dil, s_second // dil
    t = min(B_MAX_TILE, l_first)
    assert l_first % t == 0 and l_second % t == 0 and t % B_QUERY_BLOCK == 0
    n_res = min(dil, B_MAX_TILE // t)
    assert dil % n_res == 0
    hb = t // B_W_HALF
    n_hb = rows // B_W_HALF

    def cur(which):
        return pl.BlockSpec((n_res, t, gw), lambda r, i: (r, i, which))

    def prev(which):
        return pl.BlockSpec((n_res, B_W_HALF, gw), lambda r, i: (r, jnp.maximum(i * hb - 1, 0), which))

    def nxt(which):
        return pl.BlockSpec((n_res, B_W_HALF, gw), lambda r, i: (r, jnp.minimum((i + 1) * hb, n_hb - 1), which))

    kern = functools.partial(_attn_kernel, t=t, rows_first=n_first // dil, l_first=l_first, l_second=l_second)
    out_spec = pl.BlockSpec((n_res, t, gw), lambda r, i: (r, i, 0))
    return pl.pallas_call(
        kern,
        grid=(dil // n_res, rows // t),
        in_specs=[cur(0), prev(1), cur(1), nxt(1), prev(2), cur(2), nxt(2),
                  pl.BlockSpec(bias.shape, lambda r, i: (0, 0, 0, 0), pipeline_mode=pl.Buffered(1))],
        out_specs=[out_spec, out_spec],
        out_shape=[jax.ShapeDtypeStruct((dil, rows, gw), F32)] * 2,
        compiler_params=_params(2),
        name=f"band_attn_d{dil}",
    )(qkv, qkv, qkv, qkv, qkv, qkv, qkv, bias)


def _combine_kernel(x_ref, o0_ref, o1_ref, o2_ref, l0_ref, l1_ref, l2_ref, w_ref, *rest, with_ffn):
    if with_ffn:
        g_ref, wg_ref, wu_ref, wd_ref, out_ref, o_scr, l_scr, x_scr = rest
    else:
        out_ref, o_scr, l_scr = rest
    n_cb = o_scr.shape[1]
    for grp, (o_ref, l_ref) in enumerate(((o0_ref, l0_ref), (o1_ref, l1_ref), (o2_ref, l2_ref))):
        dil, rows, _ = o_ref.shape
        for r in range(dil):
            for cb in range(n_cb):
                cols = slice(cb * LANES, (cb + 1) * LANES)
                o_scr[grp, cb, pl.ds(r, rows, stride=dil), :] = o_ref[r, :, cols]
                l_scr[grp, cb, pl.ds(r, rows, stride=dil), :] = l_ref[r, :, cols]

    def full(scr, grp):
        return jnp.concatenate([scr[grp, cb] for cb in range(n_cb)], axis=1)

    l0, l1, l2 = full(l_scr, 0), full(l_scr, 1), full(l_scr, 2)
    m = jnp.maximum(jnp.maximum(l0, l1), l2)
    e0, e1, e2 = jnp.exp(l0 - m), jnp.exp(l1 - m), jnp.exp(l2 - m)
    inv = 1.0 / (e0 + e1 + e2)
    acc = x_ref[...]
    for grp, e in enumerate((e0, e1, e2)):
        acc = acc + _dot((full(o_scr, grp) * (e * inv)).astype(BF16), w_ref[grp])
    if not with_ffn:
        out_ref[...] = acc
        return
    x_scr[...] = acc
    sub = x_scr.shape[0] // COMBINE_FFN_SUBTILES
    for s in range(COMBINE_FFN_SUBTILES):
        rows = slice(s * sub, (s + 1) * sub)
        out_ref[rows, :] = _swiglu_residual(x_scr[rows, :], g_ref, wg_ref, wu_ref, wd_ref)


def _combine_proj(x, outs, lses, w_out, ffn=None, *, tm=512):
    n, d = x.shape
    gw = B_GROUP_WIDTH
    ng = len(outs)
    assert n % tm == 0
    grouped = [pl.BlockSpec((o.shape[0], tm // o.shape[0], gw), lambda i: (0, i, 0)) for o in outs]
    ins = [x, *outs, *lses, w_out.astype(BF16).reshape(ng, gw, d)]
    specs = [_rows(tm, d)] + grouped + grouped + [_resident((ng, gw, d))]
    scratch = [pltpu.VMEM((ng, gw // LANES, tm, LANES), F32)] * 2
    if ffn is not None:
        g, wg, wu, wd, layer = ffn
        ins += [g.reshape(1, d), wg, wu, wd]
        specs += [_resident((1, d)), _layer_weight(wg.shape, layer), _layer_weight(wu.shape, layer),
                  _layer_weight(wd.shape, layer)]
        scratch = scratch + [pltpu.VMEM((tm, d), F32)]
    return pl.pallas_call(
        functools.partial(_combine_kernel, with_ffn=ffn is not None),
        grid=(n // tm,),
        in_specs=specs,
        out_specs=_rows(tm, d),
        out_shape=jax.ShapeDtypeStruct((n, d), F32),
        scratch_shapes=scratch,
        compiler_params=_params(1),
        name="attn_combine",
    )(*ins)


def _dilated_attention(x, g, w_qkv, w_out, rel_bias, ffn=None, *, n_first, s_first, s_second):
    qkvs = _qkv_proj(x, g, w_qkv)
    outs, lses = [], []
    for grp, (window, dil) in enumerate(B_PATTERNS):
        assert window // (2 * dil) == B_W_HALF
        table = rel_bias[:, grp * B_HEADS_PER_GROUP:(grp + 1) * B_HEADS_PER_GROUP]
        bias = _band_bias(table, dil)
        o, l = _band_attention(qkvs[grp], bias, n_first=n_first, s_first=s_first, s_second=s_second)
        outs.append(o)
        lses.append(l)
    return _combine_proj(x, outs, lses, w_out, ffn)


def kernel(x_prompt, x_sample, norm_ffn1, ffn1_w_gate, ffn1_w_up, ffn1_w_down, norm_mix, norm_ffn2, ffn2_w_gate, ffn2_w_up, ffn2_w_down, a_w_in, a_g_v, a_w_spatial, a_b_spatial, a_w_out, b_w_qkv, b_w_out, rel_bias, c_w_pw1, c_b_pw1, c_w_dw, c_b_dw, c_g_norm, c_w_pw2, c_b_pw2, norm_final):
    bp, sp, d = x_prompt.shape
    bs, ss, _ = x_sample.shape
    n_first = bp * sp
    depth = norm_ffn1.shape[0]
    seq = dict(n_first=n_first, s_first=sp, s_second=ss)
    w1 = [w.astype(BF16) for w in (ffn1_w_gate, ffn1_w_up, ffn1_w_down)]
    w2 = [w.astype(BF16) for w in (ffn2_w_gate, ffn2_w_up, ffn2_w_down)]
    xs = [x_prompt.reshape(n_first, d), x_sample.reshape(bs * ss, d)]
    for i in range(depth):
        x = _ffn(xs if i == 0 else [x], norm_ffn1[i], *w1, i)
        kind, j = i % 3, i // 3
        if kind == 0:
            x = _gmlp(x, norm_mix[i], a_w_in[j], a_g_v[j], a_w_spatial[j], a_b_spatial[j], a_w_out[j])
        elif kind == 1:
            ffn2 = (norm_ffn2[i], *w2, i) if i < depth - 1 else None
            x = _dilated_attention(x, norm_mix[i], b_w_qkv[j], b_w_out[j], rel_bias, ffn2, **seq)
            if ffn2 is not None:
                continue
        else:
            x = _conformer(x, norm_mix[i], c_w_pw1[j], c_b_pw1[j], c_w_dw[j], c_b_dw[j], c_g_norm[j],
                           c_w_pw2[j], c_b_pw2[j], **seq)
        if i < depth - 1:
            x = _ffn([x], norm_ffn2[i], *w2, i)
        else:
            y_prompt = _ffn([x], norm_ffn2[i], *w2, i, norm_final, row_range=(0, n_first))
            y_sample = _ffn([x], norm_ffn2[i], *w2, i, norm_final, row_range=(n_first, bs * ss))
    return y_prompt.reshape(bp, sp, d), y_sample.reshape(bs, ss, d)
```

```python
import functools
import math

import jax
import jax.numpy as jnp
from jax import lax
from jax.experimental import pallas as pl
from jax.experimental.pallas import tpu as pltpu

F32 = jnp.float32
BF16 = jnp.bfloat16

EPS = 1e-6
NEG_INF = -1e30

A_CHUNK = 128
A_GROUPS = 8
A_BLOCK_GROUPS = 2
B_PATTERNS = ((128, 1), (512, 4), (2048, 16))
B_HEADS_PER_GROUP = 6
B_HEAD_DIM = 64
B_GROUP_WIDTH = B_HEADS_PER_GROUP * B_HEAD_DIM
B_W_HALF = 64
B_QUERY_BLOCK = 128
B_MAX_TILE = 1024
REL_BUCKETS = 32
REL_MAX_DIST = 1024
CONV_WIDTH = 31
CONV_PAD = CONV_WIDTH // 2
CONV_HALO = 16

V7X_VMEM_LIMIT_BYTES = 56 * 1024 * 1024
LANES = 128
SUBLANES = 8
CONV_ROWS = 256
FFN_SUBTILES = 4
COMBINE_FFN_SUBTILES = 2


def _params(n_grid_dims):
    return pltpu.CompilerParams(
        dimension_semantics=("arbitrary",) * n_grid_dims,
        vmem_limit_bytes=V7X_VMEM_LIMIT_BYTES,
    )


def _resident(shape):
    nd = len(shape)
    return pl.BlockSpec(shape, lambda *_: (0,) * nd, pipeline_mode=pl.Buffered(1))


def _rows(tm, width):
    return pl.BlockSpec((tm, width), lambda i: (i, 0))


def _rms(x, g):
    ms = jnp.mean(x * x, axis=-1, keepdims=True)
    return x * lax.rsqrt(ms + EPS) * g


def _dot(a, b):
    return jnp.dot(a, b, preferred_element_type=F32)


def _swiglu_residual(x, g_ref, wg_ref, wu_ref, wd_ref):
    h = _rms(x, g_ref[...]).astype(BF16)
    gate = _dot(h, wg_ref[...])
    up = _dot(h, wu_ref[...])
    act = (gate * jax.nn.sigmoid(gate) * up).astype(BF16)
    return x + 0.5 * _dot(act, wd_ref[...])


def _ffn_kernel(*refs, n_x, tiles_first, final_norm):
    x_refs, refs = refs[:n_x], refs[n_x:]
    g_ref, wg_ref, wu_ref, wd_ref = refs[:4]
    gf_ref = refs[4] if final_norm else None
    o_ref = refs[-1]
    sub = x_refs[0].shape[0] // FFN_SUBTILES
    for s in range(FFN_SUBTILES):
        rows = slice(s * sub, (s + 1) * sub)
        x = x_refs[0][rows, :]
        if n_x == 2:
            x = jnp.where(pl.program_id(0) < tiles_first, x, x_refs[1][rows, :])
        out = _swiglu_residual(x, g_ref, wg_ref, wu_ref, wd_ref)
        if final_norm:
            out = _rms(out, gf_ref[...])
        o_ref[rows, :] = out


def _layer_weight(shape, layer):
    return pl.BlockSpec((None,) + tuple(shape[1:]), lambda i: (layer,) + (0,) * (len(shape) - 1),
                        pipeline_mode=pl.Buffered(1))


def _ffn(xs, g, wg, wu, wd, layer, g_final=None, *, row_range=None, tm=1024):
    d = xs[0].shape[1]
    n_rows = [x.shape[0] for x in xs]
    assert all(r % tm == 0 for r in n_rows)
    tiles_first = n_rows[0] // tm
    if row_range is not None:
        start, n = row_range
        assert len(xs) == 1 and start % tm == 0 and n % tm == 0
        x_specs = [pl.BlockSpec((tm, d), lambda i: (i + start // tm, 0))]
    elif len(xs) == 1:
        n = n_rows[0]
        x_specs = [_rows(tm, d)]
    else:
        n = sum(n_rows)
        x_specs = [pl.BlockSpec((tm, d), lambda i: (jnp.minimum(i, tiles_first - 1), 0)),
                   pl.BlockSpec((tm, d), lambda i: (jnp.maximum(i - tiles_first, 0), 0))]
    ins = list(xs) + [g.reshape(1, d), wg, wu, wd]
    specs = x_specs + [_resident((1, d)), _layer_weight(wg.shape, layer), _layer_weight(wu.shape, layer),
                       _layer_weight(wd.shape, layer)]
    if g_final is not None:
        ins.append(g_final.reshape(1, d))
        specs.append(_resident((1, d)))
    kern = functools.partial(_ffn_kernel, n_x=len(xs), tiles_first=tiles_first, final_norm=g_final is not None)
    return pl.pallas_call(
        kern,
        grid=(n // tm,),
        in_specs=specs,
        out_specs=_rows(tm, d),
        out_shape=jax.ShapeDtypeStruct((n, d), F32),
        compiler_params=_params(1),
        name="ffn",
    )(*ins)


def _gmlp_kernel(x_ref, g_ref, win_ref, gv_ref, ws_ref, bs_ref, wout_ref, o_ref, v_ref, uv_ref, *, tm):
    half = gv_ref.shape[1]
    gdim = half // A_GROUPS
    width = A_BLOCK_GROUPS * gdim
    x = x_ref[...]
    h = _rms(x, g_ref[...]).astype(BF16)
    ssq = jnp.zeros((tm, 1), F32)
    for blk in range(half // width):
        cols = slice(blk * width, (blk + 1) * width)
        v = jax.nn.gelu(_dot(h, win_ref[:, half + blk * width:half + (blk + 1) * width]))
        ssq = ssq + jnp.sum(v * v, axis=-1, keepdims=True)
        v_ref[:, cols] = v
    inv = lax.rsqrt(ssq / half + EPS)
    acc = x
    for blk in range(half // width):
        cols = slice(blk * width, (blk + 1) * width)
        u = jax.nn.gelu(_dot(h, win_ref[:, cols]))
        v = (v_ref[:, cols] * inv * gv_ref[:, cols]).astype(BF16)
        for c in range(tm // A_CHUNK):
            rows = slice(c * A_CHUNK, (c + 1) * A_CHUNK)
            for g in range(A_BLOCK_GROUPS):
                gc = slice(g * gdim, (g + 1) * gdim)
                sv = _dot(ws_ref[blk * A_BLOCK_GROUPS + g], v[rows, gc]) + bs_ref[blk * A_BLOCK_GROUPS + g]
                uv_ref[blk, rows, gc] = (u[rows, gc] * sv).astype(BF16)
        acc = acc + _dot(uv_ref[blk], wout_ref[cols, :])
    o_ref[...] = acc


def _gmlp(x, g, w_in, g_v, w_sp, b_sp, w_out, *, tm=512):
    n, d = x.shape
    hid = w_in.shape[1]
    half = hid // 2
    assert n % tm == 0 and tm % A_CHUNK == 0
    return pl.pallas_call(
        functools.partial(_gmlp_kernel, tm=tm),
        grid=(n // tm,),
        in_specs=[
            _rows(tm, d), _resident((1, d)), _resident((d, hid)), _resident((1, half)),
            _resident((A_GROUPS, A_CHUNK, A_CHUNK)), _resident((A_GROUPS, A_CHUNK, 1)),
            _resident((half, d)),
        ],
        out_specs=_rows(tm, d),
        out_shape=jax.ShapeDtypeStruct((n, d), F32),
        scratch_shapes=[pltpu.VMEM((tm, half), F32),
                        pltpu.VMEM((A_GROUPS // A_BLOCK_GROUPS, tm, A_BLOCK_GROUPS * (half // A_GROUPS)), BF16)],
        compiler_params=_params(1),
        name="gmlp",
    )(x, g.reshape(1, d), w_in.astype(BF16), g_v.reshape(1, half), w_sp.astype(BF16),
      b_sp.reshape(A_GROUPS, A_CHUNK, 1), w_out.astype(BF16))


def _seq_pos(row0, n_first, s_first, s_second):
    in_first = row0 < n_first
    pos = jnp.where(in_first, row0 % s_first, (row0 - n_first) % s_second)
    return pos, jnp.where(in_first, s_first, s_second)


def _conv_kernel(xp_ref, xc_ref, xn_ref, g_ref, w1_ref, b1_ref, wdw_ref, bdw_ref, gn_ref, w2_ref, b2_ref,
                 o_ref, glu_ref, conv_ref, shift_ref, *, t, n_first, s_first, s_second):
    d = xc_ref.shape[1]
    ext = t + 2 * CONV_HALO
    pos0, seq_len = _seq_pos(pl.program_id(0) * t, n_first, s_first, s_second)
    xc = xc_ref[...]
    xa = jnp.concatenate([xp_ref[...], xc, xn_ref[...]], axis=0)
    h = _rms(xa, g_ref[...]).astype(BF16)
    p = _dot(h, w1_ref[...]) + b1_ref[...]
    glu = p[:, :d] * jax.nn.sigmoid(p[:, d:])
    pos = pos0 - CONV_HALO + lax.broadcasted_iota(jnp.int32, (ext, 1), 0)
    glu = jnp.where((pos >= 0) & (pos < seq_len), glu, 0.0)
    n_cb = d // LANES
    for cb in range(n_cb):
        glu_ref[cb] = glu[:, cb * LANES:(cb + 1) * LANES]
    base = CONV_HALO - CONV_PAD
    span = ((CONV_WIDTH - 1) // SUBLANES) * SUBLANES

    def lane_block(cb, carry):
        for b in range(SUBLANES):
            shift_ref[b] = glu_ref[cb, base + b:base + b + t + span, :]
        for rb in range(t // CONV_ROWS):
            acc = jnp.zeros((CONV_ROWS, LANES), F32)
            for j in range(CONV_WIDTH):
                r0 = rb * CONV_ROWS + j - j % SUBLANES
                acc = acc + wdw_ref[cb, j:j + 1, :] * shift_ref[j % SUBLANES, r0:r0 + CONV_ROWS, :]
            conv_ref[cb, rb * CONV_ROWS:(rb + 1) * CONV_ROWS, :] = acc
        return carry

    lax.fori_loop(0, n_cb, lane_block, 0)
    c = jnp.concatenate([conv_ref[cb] for cb in range(n_cb)], axis=1) + bdw_ref[...]
    c = _rms(c, gn_ref[...])
    c = (c * jax.nn.sigmoid(c)).astype(BF16)
    o_ref[...] = xc + _dot(c, w2_ref[...]) + b2_ref[...]


def _conformer(x, g, w1, b1, wdw, bdw, gn, w2, b2, *, n_first, s_first, s_second, t=1024):
    n, d = x.shape
    assert n % t == 0 and s_first % t == 0 and s_second % t == 0 and t % CONV_HALO == 0
    hb = t // CONV_HALO
    n_hb = n // CONV_HALO
    n_cb = d // LANES
    kern = functools.partial(_conv_kernel, t=t, n_first=n_first, s_first=s_first, s_second=s_second)
    return pl.pallas_call(
        kern,
        grid=(n // t,),
        in_specs=[
            pl.BlockSpec((CONV_HALO, d), lambda i: (jnp.maximum(i * hb - 1, 0), 0)),
            _rows(t, d),
            pl.BlockSpec((CONV_HALO, d), lambda i: (jnp.minimum((i + 1) * hb, n_hb - 1), 0)),
            _resident((1, d)), _resident((d, 2 * d)), _resident((1, 2 * d)),
            _resident((n_cb, CONV_WIDTH, LANES)), _resident((1, d)), _resident((1, d)),
            _resident((d, d)), _resident((1, d)),
        ],
        out_specs=_rows(t, d),
        out_shape=jax.ShapeDtypeStruct((n, d), F32),
        scratch_shapes=[pltpu.VMEM((n_cb, t + 2 * CONV_HALO, LANES), F32),
                        pltpu.VMEM((n_cb, t, LANES), F32),
                        pltpu.VMEM((SUBLANES, t + (CONV_WIDTH - 1) // SUBLANES * SUBLANES, LANES), F32)],
        compiler_params=_params(1),
        name="conformer",
    )(x, x, x, g.reshape(1, d), w1.astype(BF16), b1.reshape(1, 2 * d),
      wdw.reshape(CONV_WIDTH, d // LANES, LANES).transpose(1, 0, 2), bdw.reshape(1, d),
      gn.reshape(1, d), w2.astype(BF16), b2.reshape(1, d))


def _qkv_kernel(x_ref, g_ref, w_ref, *rest, tm):
    o_refs, h_ref = rest[:-1], rest[-1]
    hn = _rms(x_ref[...], g_ref[...])
    n_cb = h_ref.shape[0]
    for cb in range(n_cb):
        h_ref[cb] = hn[:, cb * LANES:(cb + 1) * LANES]
    gw = B_GROUP_WIDTH
    scale = 1.0 / math.sqrt(B_HEAD_DIM)
    for grp, (o_ref, (_, dil)) in enumerate(zip(o_refs, B_PATTERNS)):
        rows = tm // dil
        if dil == 1:
            h = hn
        else:
            h = jnp.concatenate(
                [jnp.concatenate([h_ref[cb, pl.ds(r, rows, stride=dil), :] for cb in range(n_cb)], axis=1)
                 for r in range(dil)], axis=0)
        qkv = _dot(h.astype(BF16), w_ref[grp])
        for r in range(dil):
            blk = qkv[r * rows:(r + 1) * rows]
            o_ref[r, :, :gw] = (blk[:, :gw] * scale).astype(BF16)
            o_ref[r, :, gw:] = blk[:, gw:].astype(BF16)


def _qkv_proj(x, g, w, *, tm=1024):
    n, d = x.shape
    ng = len(B_PATTERNS)
    gw = B_GROUP_WIDTH
    assert n % tm == 0 and w.shape[1] == 3 * ng * gw
    w_grp = w.astype(BF16).reshape(d, 3, ng, gw).transpose(2, 0, 1, 3).reshape(ng, d, 3 * gw)
    return pl.pallas_call(
        functools.partial(_qkv_kernel, tm=tm),
        grid=(n // tm,),
        in_specs=[_rows(tm, d), _resident((1, d)), _resident((ng, d, 3 * gw))],
        out_specs=[pl.BlockSpec((dil, tm // dil, 3 * gw), lambda i: (0, i, 0)) for _, dil in B_PATTERNS],
        out_shape=[jax.ShapeDtypeStruct((dil, n // dil, 3 * gw), BF16) for _, dil in B_PATTERNS],
        scratch_shapes=[pltpu.VMEM((d // LANES, tm, LANES), F32)],
        compiler_params=_params(1),
        name="qkv_proj",
    )(x, g.reshape(1, d), w_grp)


def _t5_bucket(rel):
    half = REL_BUCKETS // 2
    max_exact = half // 2
    ret = jnp.where(rel > 0, half, 0)
    n = jnp.abs(rel)
    nf = jnp.maximum(n, 1).astype(jnp.float32)
    large = max_exact + (jnp.log(nf / max_exact) / math.log(REL_MAX_DIST / max_exact)
                         * (half - max_exact)).astype(jnp.int32)
    large = jnp.minimum(large, half - 1)
    return ret + jnp.where(n < max_exact, n, large)


def _bias_kernel(table_ref, bucket_ref, rel_ref, o_ref):
    bucket = bucket_ref[...]
    tk = bucket.shape[1]
    in_band = jnp.abs(rel_ref[...]) <= B_W_HALF
    kk = lax.broadcasted_iota(jnp.int32, bucket.shape, 1)
    after_start, before_end = kk >= B_W_HALF, kk < tk - B_W_HALF
    masks = [in_band, in_band & after_start, in_band & before_end, in_band & after_start & before_end]
    for h in range(o_ref.shape[1]):
        acc = jnp.zeros(bucket.shape, F32)
        for b in range(REL_BUCKETS):
            acc = jnp.where(bucket == b, table_ref[b, h], acc)
        for e, mask in enumerate(masks):
            o_ref[e, h] = jnp.where(mask, acc, NEG_INF)


def _band_bias(table, dil):
    t = B_QUERY_BLOCK
    tk = t + 2 * B_W_HALF
    q_idx = jnp.arange(t, dtype=jnp.int32)
    k_idx = jnp.arange(tk, dtype=jnp.int32)
    rel = k_idx[None, :] - B_W_HALF - q_idx[:, None]
    bucket = _t5_bucket(rel * dil)
    nh = table.shape[1]
    return pl.pallas_call(
        _bias_kernel,
        in_specs=[pl.BlockSpec(memory_space=pltpu.SMEM), pl.BlockSpec((t, tk), lambda: (0, 0)),
                  pl.BlockSpec((t, tk), lambda: (0, 0))],
        out_specs=pl.BlockSpec((4, nh, t, tk), lambda: (0, 0, 0, 0)),
        out_shape=jax.ShapeDtypeStruct((4, nh, t, tk), F32),
        name="band_bias",
    )(table, bucket, rel)


def _attn_kernel(q_ref, kp_ref, kc_ref, kn_ref, vp_ref, vc_ref, vn_ref, bias_ref, o_ref, lse_ref,
                 *, t, rows_first, l_first, l_second):
    qb = B_QUERY_BLOCK
    n_qb = t // qb
    pos0, seq_len = _seq_pos(pl.program_id(1) * t, rows_first, l_first, l_second)
    at_start = (pos0 == 0).astype(jnp.int32)
    at_end = (pos0 + t == seq_len).astype(jnp.int32)
    lane = lax.broadcasted_iota(jnp.int32, (1, LANES), 1)
    low = lane < B_HEAD_DIM
    for res in range(q_ref.shape[0]):
        k = jnp.concatenate([kp_ref[res], kc_ref[res], kn_ref[res]], axis=0)
        v = jnp.concatenate([vp_ref[res], vc_ref[res], vn_ref[res]], axis=0)
        q = q_ref[res]
        for j in range(n_qb):
            edge = (at_start if j == 0 else 0) + (2 * at_end if j == n_qb - 1 else 0)
            q_rows = slice(j * qb, (j + 1) * qb)
            k_rows = slice(j * qb, (j + 1) * qb + 2 * B_W_HALF)
            for pair in range(B_GROUP_WIDTH // LANES):
                cols = slice(pair * LANES, (pair + 1) * LANES)
                qp, kp, vp = q[q_rows, cols], k[k_rows, cols], v[k_rows, cols]
                outs, lses = [], []
                for sub in range(2):
                    qh = jnp.where(low if sub == 0 else ~low, qp, jnp.zeros_like(qp))
                    s = lax.dot_general(qh, kp, (((1,), (1,)), ((), ())), preferred_element_type=F32)
                    s = s + bias_ref[edge, 2 * pair + sub]
                    m = jnp.max(s, axis=-1, keepdims=True)
                    p = jnp.exp(s - m)
                    l = jnp.sum(p, axis=-1, keepdims=True)
                    outs.append(_dot(p.astype(BF16), vp) * (1.0 / l))
                    lses.append(m + jnp.log(l))
                o_ref[res, q_rows, cols] = jnp.where(low, outs[0], outs[1])
                lse_ref[res, q_rows, cols] = jnp.where(low, lses[0], lses[1])


def _band_attention(qkv, bias, *, n_first, s_first, s_second):
    dil, rows, _ = qkv.shape
    gw = B_GROUP_WIDTH
    l_first, l_second = s_first // dil, s_second // dil
    t = min(B_MAX_TILE, l_first)
    assert l_first % t == 0 and l_second % t == 0 and t % B_QUERY_BLOCK == 0
    n_res = min(dil, B_MAX_TILE // t)
    assert dil % n_res == 0
    hb = t // B_W_HALF
    n_hb = rows // B_W_HALF

    def cur(which):
        return pl.BlockSpec((n_res, t, gw), lambda r, i: (r, i, which))

    def prev(which):
        return pl.BlockSpec((n_res, B_W_HALF, gw), lambda r, i: (r, jnp.maximum(i * hb - 1, 0), which))

    def nxt(which):
        return pl.BlockSpec((n_res, B_W_HALF, gw), lambda r, i: (r, jnp.minimum((i + 1) * hb, n_hb - 1), which))

    kern = functools.partial(_attn_kernel, t=t, rows_first=n_first // dil, l_first=l_first, l_second=l_second)
    out_spec = pl.BlockSpec((n_res, t, gw), lambda r, i: (r, i, 0))
    return pl.pallas_call(
        kern,
        grid=(dil // n_res, rows // t),
        in_specs=[cur(0), prev(1), cur(1), nxt(1), prev(2), cur(2), nxt(2),
                  pl.BlockSpec(bias.shape, lambda r, i: (0, 0, 0, 0), pipeline_mode=pl.Buffered(1))],
        out_specs=[out_spec, out_spec],
        out_shape=[jax.ShapeDtypeStruct((dil, rows, gw), F32)] * 2,
        compiler_params=_params(2),
        name=f"band_attn_d{dil}",
    )(qkv, qkv, qkv, qkv, qkv, qkv, qkv, bias)


def _combine_kernel(x_ref, o0_ref, o1_ref, o2_ref, l0_ref, l1_ref, l2_ref, w_ref, *rest, with_ffn):
    if with_ffn:
        g_ref, wg_ref, wu_ref, wd_ref, out_ref, o_scr, l_scr, x_scr = rest
    else:
        out_ref, o_scr, l_scr = rest
    n_cb = o_scr.shape[1]
    for grp, (o_ref, l_ref) in enumerate(((o0_ref, l0_ref), (o1_ref, l1_ref), (o2_ref, l2_ref))):
        dil, rows, _ = o_ref.shape
        for r in range(dil):
            for cb in range(n_cb):
                cols = slice(cb * LANES, (cb + 1) * LANES)
                o_scr[grp, cb, pl.ds(r, rows, stride=dil), :] = o_ref[r, :, cols]
                l_scr[grp, cb, pl.ds(r, rows, stride=dil), :] = l_ref[r, :, cols]

    def full(scr, grp):
        return jnp.concatenate([scr[grp, cb] for cb in range(n_cb)], axis=1)

    l0, l1, l2 = full(l_scr, 0), full(l_scr, 1), full(l_scr, 2)
    m = jnp.maximum(jnp.maximum(l0, l1), l2)
    e0, e1, e2 = jnp.exp(l0 - m), jnp.exp(l1 - m), jnp.exp(l2 - m)
    inv = 1.0 / (e0 + e1 + e2)
    acc = x_ref[...]
    for grp, e in enumerate((e0, e1, e2)):
        acc = acc + _dot((full(o_scr, grp) * (e * inv)).astype(BF16), w_ref[grp])
    if not with_ffn:
        out_ref[...] = acc
        return
    x_scr[...] = acc
    sub = x_scr.shape[0] // COMBINE_FFN_SUBTILES
    for s in range(COMBINE_FFN_SUBTILES):
        rows = slice(s * sub, (s + 1) * sub)
        out_ref[rows, :] = _swiglu_residual(x_scr[rows, :], g_ref, wg_ref, wu_ref, wd_ref)


def _combine_proj(x, outs, lses, w_out, ffn=None, *, tm=512):
    n, d = x.shape
    gw = B_GROUP_WIDTH
    ng = len(outs)
    assert n % tm == 0
    grouped = [pl.BlockSpec((o.shape[0], tm // o.shape[0], gw), lambda i: (0, i, 0)) for o in outs]
    ins = [x, *outs, *lses, w_out.astype(BF16).reshape(ng, gw, d)]
    specs = [_rows(tm, d)] + grouped + grouped + [_resident((ng, gw, d))]
    scratch = [pltpu.VMEM((ng, gw // LANES, tm, LANES), F32)] * 2
    if ffn is not None:
        g, wg, wu, wd, layer = ffn
        ins += [g.reshape(1, d), wg, wu, wd]
        specs += [_resident((1, d)), _layer_weight(wg.shape, layer), _layer_weight(wu.shape, layer),
                  _layer_weight(wd.shape, layer)]
        scratch = scratch + [pltpu.VMEM((tm, d), F32)]
    return pl.pallas_call(
        functools.partial(_combine_kernel, with_ffn=ffn is not None),
        grid=(n // tm,),
        in_specs=specs,
        out_specs=_rows(tm, d),
        out_shape=jax.ShapeDtypeStruct((n, d), F32),
        scratch_shapes=scratch,
        compiler_params=_params(1),
        name="attn_combine",
    )(*ins)


def _dilated_attention(x, g, w_qkv, w_out, rel_bias, ffn=None, *, n_first, s_first, s_second):
    qkvs = _qkv_proj(x, g, w_qkv)
    outs, lses = [], []
    for grp, (window, dil) in enumerate(B_PATTERNS):
        assert window // (2 * dil) == B_W_HALF
        table = rel_bias[:, grp * B_HEADS_PER_GROUP:(grp + 1) * B_HEADS_PER_GROUP]
        bias = _band_bias(table, dil)
        o, l = _band_attention(qkvs[grp], bias, n_first=n_first, s_first=s_first, s_second=s_second)
        outs.append(o)
        lses.append(l)
    return _combine_proj(x, outs, lses, w_out, ffn)


def kernel(x_prompt, x_sample, norm_ffn1, ffn1_w_gate, ffn1_w_up, ffn1_w_down, norm_mix, norm_ffn2, ffn2_w_gate, ffn2_w_up, ffn2_w_down, a_w_in, a_g_v, a_w_spatial, a_b_spatial, a_w_out, b_w_qkv, b_w_out, rel_bias, c_w_pw1, c_b_pw1, c_w_dw, c_b_dw, c_g_norm, c_w_pw2, c_b_pw2, norm_final):
    bp, sp, d = x_prompt.shape
    bs, ss, _ = x_sample.shape
    n_first = bp * sp
    depth = norm_ffn1.shape[0]
    seq = dict(n_first=n_first, s_first=sp, s_second=ss)
    w1 = [w.astype(BF16) for w in (ffn1_w_gate, ffn1_w_up, ffn1_w_down)]
    w2 = [w.astype(BF16) for w in (ffn2_w_gate, ffn2_w_up, ffn2_w_down)]
    xs = [x_prompt.reshape(n_first, d), x_sample.reshape(bs * ss, d)]
    for i in range(depth):
        x = _ffn(xs if i == 0 else [x], norm_ffn1[i], *w1, i)
        kind, j = i % 3, i // 3
        if kind == 0:
            x = _gmlp(x, norm_mix[i], a_w_in[j], a_g_v[j], a_w_spatial[j], a_b_spatial[j], a_w_out[j])
        elif kind == 1:
            ffn2 = (norm_ffn2[i], *w2, i) if i < depth - 1 else None
            x = _dilated_attention(x, norm_mix[i], b_w_qkv[j], b_w_out[j], rel_bias, ffn2, **seq)
            if ffn2 is not None:
                continue
        else:
            x = _conformer(x, norm_mix[i], c_w_pw1[j], c_b_pw1[j], c_w_dw[j], c_b_dw[j], c_g_norm[j],
                           c_w_pw2[j], c_b_pw2[j], **seq)
        if i < depth - 1:
            x = _ffn([x], norm_ffn2[i], *w2, i)
        else:
            y_prompt = _ffn([x], norm_ffn2[i], *w2, i, norm_final, row_range=(0, n_first))
            y_sample = _ffn([x], norm_ffn2[i], *w2, i, norm_final, row_range=(n_first, bs * ss))
    return y_prompt.reshape(bp, sp, d), y_sample.reshape(bs, ss, d)
```

```python
import functools
import math

import jax
import jax.numpy as jnp
from jax import lax
from jax.experimental import pallas as pl
from jax.experimental.pallas import tpu as pltpu

F32 = jnp.float32
BF16 = jnp.bfloat16

EPS = 1e-6
NEG_INF = -1e30

A_CHUNK = 128
A_GROUPS = 8
A_BLOCK_GROUPS = 2
B_PATTERNS = ((128, 1), (512, 4), (2048, 16))
B_HEADS_PER_GROUP = 6
B_HEAD_DIM = 64
B_GROUP_WIDTH = B_HEADS_PER_GROUP * B_HEAD_DIM
B_W_HALF = 64
B_QUERY_BLOCK = 128
B_MAX_TILE = 1024
REL_BUCKETS = 32
REL_MAX_DIST = 1024
CONV_WIDTH = 31
CONV_PAD = CONV_WIDTH // 2
CONV_HALO = 16

V7X_VMEM_LIMIT_BYTES = 56 * 1024 * 1024
LANES = 128
SUBLANES = 8
CONV_ROWS = 256
FFN_SUBTILES = 4
COMBINE_FFN_SUBTILES = 2


def _params(n_grid_dims):
    return pltpu.CompilerParams(
        dimension_semantics=("arbitrary",) * n_grid_dims,
        vmem_limit_bytes=V7X_VMEM_LIMIT_BYTES,
    )


def _resident(shape):
    nd = len(shape)
    return pl.BlockSpec(shape, lambda *_: (0,) * nd, pipeline_mode=pl.Buffered(1))


def _rows(tm, width):
    return pl.BlockSpec((tm, width), lambda i: (i, 0))


def _rms(x, g):
    ms = jnp.mean(x * x, axis=-1, keepdims=True)
    return x * lax.rsqrt(ms + EPS) * g


def _dot(a, b):
    return jnp.dot(a, b, preferred_element_type=F32)


def _swiglu_residual(x, g_ref, wg_ref, wu_ref, wd_ref):
    h = _rms(x, g_ref[...]).astype(BF16)
    gate = _dot(h, wg_ref[...])
    up = _dot(h, wu_ref[...])
    act = (gate * jax.nn.sigmoid(gate) * up).astype(BF16)
    return x + 0.5 * _dot(act, wd_ref[...])


def _ffn_kernel(*refs, n_x, tiles_first, final_norm):
    x_refs, refs = refs[:n_x], refs[n_x:]
    g_ref, wg_ref, wu_ref, wd_ref = refs[:4]
    gf_ref = refs[4] if final_norm else None
    o_ref = refs[-1]
    sub = x_refs[0].shape[0] // FFN_SUBTILES
    for s in range(FFN_SUBTILES):
        rows = slice(s * sub, (s + 1) * sub)
        x = x_refs[0][rows, :]
        if n_x == 2:
            x = jnp.where(pl.program_id(0) < tiles_first, x, x_refs[1][rows, :])
        out = _swiglu_residual(x, g_ref, wg_ref, wu_ref, wd_ref)
        if final_norm:
            out = _rms(out, gf_ref[...])
        o_ref[rows, :] = out


def _layer_weight(shape, layer):
    return pl.BlockSpec((None,) + tuple(shape[1:]), lambda i: (layer,) + (0,) * (len(shape) - 1),
                        pipeline_mode=pl.Buffered(1))


def _ffn(xs, g, wg, wu, wd, layer, g_final=None, *, row_range=None, tm=1024):
    d = xs[0].shape[1]
    n_rows = [x.shape[0] for x in xs]
    assert all(r % tm == 0 for r in n_rows)
    tiles_first = n_rows[0] // tm
    if row_range is not None:
        start, n = row_range
        assert len(xs) == 1 and start % tm == 0 and n % tm == 0
        x_specs = [pl.BlockSpec((tm, d), lambda i: (i + start // tm, 0))]
    elif len(xs) == 1:
        n = n_rows[0]
        x_specs = [_rows(tm, d)]
    else:
        n = sum(n_rows)
        x_specs = [pl.BlockSpec((tm, d), lambda i: (jnp.minimum(i, tiles_first - 1), 0)),
                   pl.BlockSpec((tm, d), lambda i: (jnp.maximum(i - tiles_first, 0), 0))]
    ins = list(xs) + [g.reshape(1, d), wg, wu, wd]
    specs = x_specs + [_resident((1, d)), _layer_weight(wg.shape, layer), _layer_weight(wu.shape, layer),
                       _layer_weight(wd.shape, layer)]
    if g_final is not None:
        ins.append(g_final.reshape(1, d))
        specs.append(_resident((1, d)))
    kern = functools.partial(_ffn_kernel, n_x=len(xs), tiles_first=tiles_first, final_norm=g_final is not None)
    return pl.pallas_call(
        kern,
        grid=(n // tm,),
        in_specs=specs,
        out_specs=_rows(tm, d),
        out_shape=jax.ShapeDtypeStruct((n, d), F32),
        compiler_params=_params(1),
        name="ffn",
    )(*ins)


def _gmlp_kernel(x_ref, g_ref, win_ref, gv_ref, ws_ref, bs_ref, wout_ref, o_ref, v_ref, uv_ref, *, tm):
    half = gv_ref.shape[1]
    gdim = half // A_GROUPS
    width = A_BLOCK_GROUPS * gdim
    x = x_ref[...]
    h = _rms(x, g_ref[...]).astype(BF16)
    ssq = jnp.zeros((tm, 1), F32)
    for blk in range(half // width):
        cols = slice(blk * width, (blk + 1) * width)
        v = jax.nn.gelu(_dot(h, win_ref[:, half + blk * width:half + (blk + 1) * width]))
        ssq = ssq + jnp.sum(v * v, axis=-1, keepdims=True)
        v_ref[:, cols] = v
    inv = lax.rsqrt(ssq / half + EPS)
    acc = x
    for blk in range(half // width):
        cols = slice(blk * width, (blk + 1) * width)
        u = jax.nn.gelu(_dot(h, win_ref[:, cols]))
        v = (v_ref[:, cols] * inv * gv_ref[:, cols]).astype(BF16)
        for c in range(tm // A_CHUNK):
            rows = slice(c * A_CHUNK, (c + 1) * A_CHUNK)
            for g in range(A_BLOCK_GROUPS):
                gc = slice(g * gdim, (g + 1) * gdim)
                sv = _dot(ws_ref[blk * A_BLOCK_GROUPS + g], v[rows, gc]) + bs_ref[blk * A_BLOCK_GROUPS + g]
                uv_ref[blk, rows, gc] = (u[rows, gc] * sv).astype(BF16)
        acc = acc + _dot(uv_ref[blk], wout_ref[cols, :])
    o_ref[...] = acc


def _gmlp(x, g, w_in, g_v, w_sp, b_sp, w_out, *, tm=768):
    n, d = x.shape
    hid = w_in.shape[1]
    half = hid // 2
    assert n % tm == 0 and tm % A_CHUNK == 0
    return pl.pallas_call(
        functools.partial(_gmlp_kernel, tm=tm),
        grid=(n // tm,),
        in_specs=[
            _rows(tm, d), _resident((1, d)), _resident((d, hid)), _resident((1, half)),
            _resident((A_GROUPS, A_CHUNK, A_CHUNK)), _resident((A_GROUPS, A_CHUNK, 1)),
            _resident((half, d)),
        ],
        out_specs=_rows(tm, d),
        out_shape=jax.ShapeDtypeStruct((n, d), F32),
        scratch_shapes=[pltpu.VMEM((tm, half), F32),
                        pltpu.VMEM((A_GROUPS // A_BLOCK_GROUPS, tm, A_BLOCK_GROUPS * (half // A_GROUPS)), BF16)],
        compiler_params=_params(1),
        name="gmlp",
    )(x, g.reshape(1, d), w_in.astype(BF16), g_v.reshape(1, half), w_sp.astype(BF16),
      b_sp.reshape(A_GROUPS, A_CHUNK, 1), w_out.astype(BF16))


def _seq_pos(row0, n_first, s_first, s_second):
    in_first = row0 < n_first
    pos = jnp.where(in_first, row0 % s_first, (row0 - n_first) % s_second)
    return pos, jnp.where(in_first, s_first, s_second)


def _conv_kernel(xp_ref, xc_ref, xn_ref, g_ref, w1_ref, b1_ref, wdw_ref, bdw_ref, gn_ref, w2_ref, b2_ref,
                 o_ref, glu_ref, conv_ref, shift_ref, *, t, n_first, s_first, s_second):
    d = xc_ref.shape[1]
    ext = t + 2 * CONV_HALO
    pos0, seq_len = _seq_pos(pl.program_id(0) * t, n_first, s_first, s_second)
    xc = xc_ref[...]
    xa = jnp.concatenate([xp_ref[...], xc, xn_ref[...]], axis=0)
    h = _rms(xa, g_ref[...]).astype(BF16)
    p = _dot(h, w1_ref[...]) + b1_ref[...]
    glu = p[:, :d] * jax.nn.sigmoid(p[:, d:])
    pos = pos0 - CONV_HALO + lax.broadcasted_iota(jnp.int32, (ext, 1), 0)
    glu = jnp.where((pos >= 0) & (pos < seq_len), glu, 0.0)
    n_cb = d // LANES
    for cb in range(n_cb):
        glu_ref[cb] = glu[:, cb * LANES:(cb + 1) * LANES]
    base = CONV_HALO - CONV_PAD
    span = ((CONV_WIDTH - 1) // SUBLANES) * SUBLANES

    def lane_block(cb, carry):
        for b in range(SUBLANES):
            shift_ref[b] = glu_ref[cb, base + b:base + b + t + span, :]
        for rb in range(t // CONV_ROWS):
            acc = jnp.zeros((CONV_ROWS, LANES), F32)
            for j in range(CONV_WIDTH):
                r0 = rb * CONV_ROWS + j - j % SUBLANES
                acc = acc + wdw_ref[cb, j:j + 1, :] * shift_ref[j % SUBLANES, r0:r0 + CONV_ROWS, :]
            conv_ref[cb, rb * CONV_ROWS:(rb + 1) * CONV_ROWS, :] = acc
        return carry

    lax.fori_loop(0, n_cb, lane_block, 0)
    c = jnp.concatenate([conv_ref[cb] for cb in range(n_cb)], axis=1) + bdw_ref[...]
    c = _rms(c, gn_ref[...])
    c = (c * jax.nn.sigmoid(c)).astype(BF16)
    o_ref[...] = xc + _dot(c, w2_ref[...]) + b2_ref[...]


def _conformer(x, g, w1, b1, wdw, bdw, gn, w2, b2, *, n_first, s_first, s_second, t=1024):
    n, d = x.shape
    assert n % t == 0 and s_first % t == 0 and s_second % t == 0 and t % CONV_HALO == 0
    hb = t // CONV_HALO
    n_hb = n // CONV_HALO
    n_cb = d // LANES
    kern = functools.partial(_conv_kernel, t=t, n_first=n_first, s_first=s_first, s_second=s_second)
    return pl.pallas_call(
        kern,
        grid=(n // t,),
        in_specs=[
            pl.BlockSpec((CONV_HALO, d), lambda i: (jnp.maximum(i * hb - 1, 0), 0)),
            _rows(t, d),
            pl.BlockSpec((CONV_HALO, d), lambda i: (jnp.minimum((i + 1) * hb, n_hb - 1), 0)),
            _resident((1, d)), _resident((d, 2 * d)), _resident((1, 2 * d)),
            _resident((n_cb, CONV_WIDTH, LANES)), _resident((1, d)), _resident((1, d)),
            _resident((d, d)), _resident((1, d)),
        ],
        out_specs=_rows(t, d),
        out_shape=jax.ShapeDtypeStruct((n, d), F32),
        scratch_shapes=[pltpu.VMEM((n_cb, t + 2 * CONV_HALO, LANES), F32),
                        pltpu.VMEM((n_cb, t, LANES), F32),
                        pltpu.VMEM((SUBLANES, t + (CONV_WIDTH - 1) // SUBLANES * SUBLANES, LANES), F32)],
        compiler_params=_params(1),
        name="conformer",
    )(x, x, x, g.reshape(1, d), w1.astype(BF16), b1.reshape(1, 2 * d),
      wdw.reshape(CONV_WIDTH, d // LANES, LANES).transpose(1, 0, 2), bdw.reshape(1, d),
      gn.reshape(1, d), w2.astype(BF16), b2.reshape(1, d))


def _qkv_kernel(x_ref, g_ref, w_ref, *rest, tm):
    o_refs, h_ref = rest[:-1], rest[-1]
    hn = _rms(x_ref[...], g_ref[...])
    n_cb = h_ref.shape[0]
    for cb in range(n_cb):
        h_ref[cb] = hn[:, cb * LANES:(cb + 1) * LANES]
    gw = B_GROUP_WIDTH
    scale = 1.0 / math.sqrt(B_HEAD_DIM)
    for grp, (o_ref, (_, dil)) in enumerate(zip(o_refs, B_PATTERNS)):
        rows = tm // dil
        if dil == 1:
            h = hn
        else:
            h = jnp.concatenate(
                [jnp.concatenate([h_ref[cb, pl.ds(r, rows, stride=dil), :] for cb in range(n_cb)], axis=1)
                 for r in range(dil)], axis=0)
        qkv = _dot(h.astype(BF16), w_ref[grp])
        for r in range(dil):
            blk = qkv[r * rows:(r + 1) * rows]
            o_ref[r, :, :gw] = (blk[:, :gw] * scale).astype(BF16)
            o_ref[r, :, gw:] = blk[:, gw:].astype(BF16)


def _qkv_proj(x, g, w, *, tm=1024):
    n, d = x.shape
    ng = len(B_PATTERNS)
    gw = B_GROUP_WIDTH
    assert n % tm == 0 and w.shape[1] == 3 * ng * gw
    w_grp = w.astype(BF16).reshape(d, 3, ng, gw).transpose(2, 0, 1, 3).reshape(ng, d, 3 * gw)
    return pl.pallas_call(
        functools.partial(_qkv_kernel, tm=tm),
        grid=(n // tm,),
        in_specs=[_rows(tm, d), _resident((1, d)), _resident((ng, d, 3 * gw))],
        out_specs=[pl.BlockSpec((dil, tm // dil, 3 * gw), lambda i: (0, i, 0)) for _, dil in B_PATTERNS],
        out_shape=[jax.ShapeDtypeStruct((dil, n // dil, 3 * gw), BF16) for _, dil in B_PATTERNS],
        scratch_shapes=[pltpu.VMEM((d // LANES, tm, LANES), F32)],
        compiler_params=_params(1),
        name="qkv_proj",
    )(x, g.reshape(1, d), w_grp)


def _t5_bucket(rel):
    half = REL_BUCKETS // 2
    max_exact = half // 2
    ret = jnp.where(rel > 0, half, 0)
    n = jnp.abs(rel)
    nf = jnp.maximum(n, 1).astype(jnp.float32)
    large = max_exact + (jnp.log(nf / max_exact) / math.log(REL_MAX_DIST / max_exact)
                         * (half - max_exact)).astype(jnp.int32)
    large = jnp.minimum(large, half - 1)
    return ret + jnp.where(n < max_exact, n, large)


def _bias_kernel(table_ref, bucket_ref, rel_ref, o_ref):
    bucket = bucket_ref[...]
    tk = bucket.shape[1]
    in_band = jnp.abs(rel_ref[...]) <= B_W_HALF
    kk = lax.broadcasted_iota(jnp.int32, bucket.shape, 1)
    after_start, before_end = kk >= B_W_HALF, kk < tk - B_W_HALF
    masks = [in_band, in_band & after_start, in_band & before_end, in_band & after_start & before_end]
    for h in range(o_ref.shape[1]):
        acc = jnp.zeros(bucket.shape, F32)
        for b in range(REL_BUCKETS):
            acc = jnp.where(bucket == b, table_ref[b, h], acc)
        for e, mask in enumerate(masks):
            o_ref[e, h] = jnp.where(mask, acc, NEG_INF)


def _band_bias(table, dil):
    t = B_QUERY_BLOCK
    tk = t + 2 * B_W_HALF
    q_idx = jnp.arange(t, dtype=jnp.int32)
    k_idx = jnp.arange(tk, dtype=jnp.int32)
    rel = k_idx[None, :] - B_W_HALF - q_idx[:, None]
    bucket = _t5_bucket(rel * dil)
    nh = table.shape[1]
    return pl.pallas_call(
        _bias_kernel,
        in_specs=[pl.BlockSpec(memory_space=pltpu.SMEM), pl.BlockSpec((t, tk), lambda: (0, 0)),
                  pl.BlockSpec((t, tk), lambda: (0, 0))],
        out_specs=pl.BlockSpec((4, nh, t, tk), lambda: (0, 0, 0, 0)),
        out_shape=jax.ShapeDtypeStruct((4, nh, t, tk), F32),
        name="band_bias",
    )(table, bucket, rel)


def _attn_kernel(q_ref, kp_ref, kc_ref, kn_ref, vp_ref, vc_ref, vn_ref, bias_ref, o_ref, lse_ref,
                 *, t, rows_first, l_first, l_second):
    qb = B_QUERY_BLOCK
    n_qb = t // qb
    pos0, seq_len = _seq_pos(pl.program_id(1) * t, rows_first, l_first, l_second)
    at_start = (pos0 == 0).astype(jnp.int32)
    at_end = (pos0 + t == seq_len).astype(jnp.int32)
    lane = lax.broadcasted_iota(jnp.int32, (1, LANES), 1)
    low = lane < B_HEAD_DIM
    for res in range(q_ref.shape[0]):
        k = jnp.concatenate([kp_ref[res], kc_ref[res], kn_ref[res]], axis=0)
        v = jnp.concatenate([vp_ref[res], vc_ref[res], vn_ref[res]], axis=0)
        q = q_ref[res]
        for j in range(n_qb):
            edge = (at_start if j == 0 else 0) + (2 * at_end if j == n_qb - 1 else 0)
            q_rows = slice(j * qb, (j + 1) * qb)
            k_rows = slice(j * qb, (j + 1) * qb + 2 * B_W_HALF)
            for pair in range(B_GROUP_WIDTH // LANES):
                cols = slice(pair * LANES, (pair + 1) * LANES)
                qp, kp, vp = q[q_rows, cols], k[k_rows, cols], v[k_rows, cols]
                outs, lses = [], []
                for sub in range(2):
                    qh = jnp.where(low if sub == 0 else ~low, qp, jnp.zeros_like(qp))
                    s = lax.dot_general(qh, kp, (((1,), (1,)), ((), ())), preferred_element_type=F32)
                    s = s + bias_ref[edge, 2 * pair + sub]
                    m = jnp.max(s, axis=-1, keepdims=True)
                    p = jnp.exp(s - m)
                    l = jnp.sum(p, axis=-1, keepdims=True)
                    outs.append(_dot(p.astype(BF16), vp) * (1.0 / l))
                    lses.append(m + jnp.log(l))
                o_ref[res, q_rows, cols] = jnp.where(low, outs[0], outs[1])
                lse_ref[res, q_rows, cols] = jnp.where(low, lses[0], lses[1])


def _band_attention(qkv, bias, *, n_first, s_first, s_second):
    dil, rows, _ = qkv.shape
    gw = B_GROUP_WIDTH
    l_first, l_second = s_first // dil, s_second // dil
    t = min(B_MAX_TILE, l_first)
    assert l_first % t == 0 and l_second % t == 0 and t % B_QUERY_BLOCK == 0
    n_res = min(dil, B_MAX_TILE // t)
    assert dil % n_res == 0
    hb = t // B_W_HALF
    n_hb = rows // B_W_HALF

    def cur(which):
        return pl.BlockSpec((n_res, t, gw), lambda r, i: (r, i, which))

    def prev(which):
        return pl.BlockSpec((n_res, B_W_HALF, gw), lambda r, i: (r, jnp.maximum(i * hb - 1, 0), which))

    def nxt(which):
        return pl.BlockSpec((n_res, B_W_HALF, gw), lambda r, i: (r, jnp.minimum((i + 1) * hb, n_hb - 1), which))

    kern = functools.partial(_attn_kernel, t=t, rows_first=n_first // dil, l_first=l_first, l_second=l_second)
    out_spec = pl.BlockSpec((n_res, t, gw), lambda r, i: (r, i, 0))
    return pl.pallas_call(
        kern,
        grid=(dil // n_res, rows // t),
        in_specs=[cur(0), prev(1), cur(1), nxt(1), prev(2), cur(2), nxt(2),
                  pl.BlockSpec(bias.shape, lambda r, i: (0, 0, 0, 0), pipeline_mode=pl.Buffered(1))],
        out_specs=[out_spec, out_spec],
        out_shape=[jax.ShapeDtypeStruct((dil, rows, gw), F32)] * 2,
        compiler_params=_params(2),
        name=f"band_attn_d{dil}",
    )(qkv, qkv, qkv, qkv, qkv, qkv, qkv, bias)


def _combine_kernel(x_ref, o0_ref, o1_ref, o2_ref, l0_ref, l1_ref, l2_ref, w_ref, *rest, with_ffn):
    if with_ffn:
        g_ref, wg_ref, wu_ref, wd_ref, out_ref, o_scr, l_scr, x_scr = rest
    else:
        out_ref, o_scr, l_scr = rest
    n_cb = o_scr.shape[1]
    for grp, (o_ref, l_ref) in enumerate(((o0_ref, l0_ref), (o1_ref, l1_ref), (o2_ref, l2_ref))):
        dil, rows, _ = o_ref.shape
        for r in range(dil):
            for cb in range(n_cb):
                cols = slice(cb * LANES, (cb + 1) * LANES)
                o_scr[grp, cb, pl.ds(r, rows, stride=dil), :] = o_ref[r, :, cols]
                l_scr[grp, cb, pl.ds(r, rows, stride=dil), :] = l_ref[r, :, cols]

    def full(scr, grp):
        return jnp.concatenate([scr[grp, cb] for cb in range(n_cb)], axis=1)

    l0, l1, l2 = full(l_scr, 0), full(l_scr, 1), full(l_scr, 2)
    m = jnp.maximum(jnp.maximum(l0, l1), l2)
    e0, e1, e2 = jnp.exp(l0 - m), jnp.exp(l1 - m), jnp.exp(l2 - m)
    inv = 1.0 / (e0 + e1 + e2)
    acc = x_ref[...]
    for grp, e in enumerate((e0, e1, e2)):
        acc = acc + _dot((full(o_scr, grp) * (e * inv)).astype(BF16), w_ref[grp])
    if not with_ffn:
        out_ref[...] = acc
        return
    x_scr[...] = acc
    sub = x_scr.shape[0] // COMBINE_FFN_SUBTILES
    for s in range(COMBINE_FFN_SUBTILES):
        rows = slice(s * sub, (s + 1) * sub)
        out_ref[rows, :] = _swiglu_residual(x_scr[rows, :], g_ref, wg_ref, wu_ref, wd_ref)


def _combine_proj(x, outs, lses, w_out, ffn=None, *, tm=512):
    n, d = x.shape
    gw = B_GROUP_WIDTH
    ng = len(outs)
    assert n % tm == 0
    grouped = [pl.BlockSpec((o.shape[0], tm // o.shape[0], gw), lambda i: (0, i, 0)) for o in outs]
    ins = [x, *outs, *lses, w_out.astype(BF16).reshape(ng, gw, d)]
    specs = [_rows(tm, d)] + grouped + grouped + [_resident((ng, gw, d))]
    scratch = [pltpu.VMEM((ng, gw // LANES, tm, LANES), F32)] * 2
    if ffn is not None:
        g, wg, wu, wd, layer = ffn
        ins += [g.reshape(1, d), wg, wu, wd]
        specs += [_resident((1, d)), _layer_weight(wg.shape, layer), _layer_weight(wu.shape, layer),
                  _layer_weight(wd.shape, layer)]
        scratch = scratch + [pltpu.VMEM((tm, d), F32)]
    return pl.pallas_call(
        functools.partial(_combine_kernel, with_ffn=ffn is not None),
        grid=(n // tm,),
        in_specs=specs,
        out_specs=_rows(tm, d),
        out_shape=jax.ShapeDtypeStruct((n, d), F32),
        scratch_shapes=scratch,
        compiler_params=_params(1),
        name="attn_combine",
    )(*ins)


def _dilated_attention(x, g, w_qkv, w_out, rel_bias, ffn=None, *, n_first, s_first, s_second):
    qkvs = _qkv_proj(x, g, w_qkv)
    outs, lses = [], []
    for grp, (window, dil) in enumerate(B_PATTERNS):
        assert window // (2 * dil) == B_W_HALF
        table = rel_bias[:, grp * B_HEADS_PER_GROUP:(grp + 1) * B_HEADS_PER_GROUP]
        bias = _band_bias(table, dil)
        o, l = _band_attention(qkvs[grp], bias, n_first=n_first, s_first=s_first, s_second=s_second)
        outs.append(o)
        lses.append(l)
    return _combine_proj(x, outs, lses, w_out, ffn)


def kernel(x_prompt, x_sample, norm_ffn1, ffn1_w_gate, ffn1_w_up, ffn1_w_down, norm_mix, norm_ffn2, ffn2_w_gate, ffn2_w_up, ffn2_w_down, a_w_in, a_g_v, a_w_spatial, a_b_spatial, a_w_out, b_w_qkv, b_w_out, rel_bias, c_w_pw1, c_b_pw1, c_w_dw, c_b_dw, c_g_norm, c_w_pw2, c_b_pw2, norm_final):
    bp, sp, d = x_prompt.shape
    bs, ss, _ = x_sample.shape
    n_first = bp * sp
    depth = norm_ffn1.shape[0]
    seq = dict(n_first=n_first, s_first=sp, s_second=ss)
    w1 = [w.astype(BF16) for w in (ffn1_w_gate, ffn1_w_up, ffn1_w_down)]
    w2 = [w.astype(BF16) for w in (ffn2_w_gate, ffn2_w_up, ffn2_w_down)]
    xs = [x_prompt.reshape(n_first, d), x_sample.reshape(bs * ss, d)]
    for i in range(depth):
        x = _ffn(xs if i == 0 else [x], norm_ffn1[i], *w1, i)
        kind, j = i % 3, i // 3
        if kind == 0:
            x = _gmlp(x, norm_mix[i], a_w_in[j], a_g_v[j], a_w_spatial[j], a_b_spatial[j], a_w_out[j])
        elif kind == 1:
            ffn2 = (norm_ffn2[i], *w2, i) if i < depth - 1 else None
            x = _dilated_attention(x, norm_mix[i], b_w_qkv[j], b_w_out[j], rel_bias, ffn2, **seq)
            if ffn2 is not None:
                continue
        else:
            x = _conformer(x, norm_mix[i], c_w_pw1[j], c_b_pw1[j], c_w_dw[j], c_b_dw[j], c_g_norm[j],
                           c_w_pw2[j], c_b_pw2[j], **seq)
        if i < depth - 1:
            x = _ffn([x], norm_ffn2[i], *w2, i)
        else:
            y_prompt = _ffn([x], norm_ffn2[i], *w2, i, norm_final, row_range=(0, n_first))
            y_sample = _ffn([x], norm_ffn2[i], *w2, i, norm_final, row_range=(n_first, bs * ss))
    return y_prompt.reshape(bp, sp, d), y_sample.reshape(bs, ss, d)
```

```python
import functools
import math

import jax
import jax.numpy as jnp
from jax import lax
from jax.experimental import pallas as pl
from jax.experimental.pallas import tpu as pltpu

F32 = jnp.float32
BF16 = jnp.bfloat16

EPS = 1e-6
NEG_INF = -1e30

A_CHUNK = 128
A_GROUPS = 8
A_BLOCK_GROUPS = 2
B_PATTERNS = ((128, 1), (512, 4), (2048, 16))
B_HEADS_PER_GROUP = 6
B_HEAD_DIM = 64
B_GROUP_WIDTH = B_HEADS_PER_GROUP * B_HEAD_DIM
B_W_HALF = 64
B_QUERY_BLOCK = 128
B_MAX_TILE = 1024
REL_BUCKETS = 32
REL_MAX_DIST = 1024
CONV_WIDTH = 31
CONV_PAD = CONV_WIDTH // 2
CONV_HALO = 16

V7X_VMEM_LIMIT_BYTES = 56 * 1024 * 1024
LANES = 128
SUBLANES = 8
CONV_ROWS = 256
CONV_PIECES = 4
FFN_SUBTILES = 4
COMBINE_FFN_SUBTILES = 2


def _params(n_grid_dims):
    return pltpu.CompilerParams(
        dimension_semantics=("arbitrary",) * n_grid_dims,
        vmem_limit_bytes=V7X_VMEM_LIMIT_BYTES,
    )


def _resident(shape):
    nd = len(shape)
    return pl.BlockSpec(shape, lambda *_: (0,) * nd, pipeline_mode=pl.Buffered(1))


def _rows(tm, width):
    return pl.BlockSpec((tm, width), lambda i: (i, 0))


def _rms(x, g):
    ms = jnp.mean(x * x, axis=-1, keepdims=True)
    return x * lax.rsqrt(ms + EPS) * g


def _dot(a, b):
    return jnp.dot(a, b, preferred_element_type=F32)


def _swiglu_residual(x, g_ref, wg_ref, wu_ref, wd_ref):
    h = _rms(x, g_ref[...]).astype(BF16)
    gate = _dot(h, wg_ref[...])
    up = _dot(h, wu_ref[...])
    act = (gate * jax.nn.sigmoid(gate) * up).astype(BF16)
    return x + 0.5 * _dot(act, wd_ref[...])


def _ffn_kernel(*refs, n_x, tiles_first, final_norm):
    x_refs, refs = refs[:n_x], refs[n_x:]
    g_ref, wg_ref, wu_ref, wd_ref = refs[:4]
    gf_ref = refs[4] if final_norm else None
    o_ref = refs[-1]
    sub = x_refs[0].shape[0] // FFN_SUBTILES
    for s in range(FFN_SUBTILES):
        rows = slice(s * sub, (s + 1) * sub)
        x = x_refs[0][rows, :]
        if n_x == 2:
            x = jnp.where(pl.program_id(0) < tiles_first, x, x_refs[1][rows, :])
        out = _swiglu_residual(x, g_ref, wg_ref, wu_ref, wd_ref)
        if final_norm:
            out = _rms(out, gf_ref[...])
        o_ref[rows, :] = out


def _layer_weight(shape, layer):
    return pl.BlockSpec((None,) + tuple(shape[1:]), lambda i: (layer,) + (0,) * (len(shape) - 1),
                        pipeline_mode=pl.Buffered(1))


def _ffn(xs, g, wg, wu, wd, layer, g_final=None, *, row_range=None, tm=1024):
    d = xs[0].shape[1]
    n_rows = [x.shape[0] for x in xs]
    assert all(r % tm == 0 for r in n_rows)
    tiles_first = n_rows[0] // tm
    if row_range is not None:
        start, n = row_range
        assert len(xs) == 1 and start % tm == 0 and n % tm == 0
        x_specs = [pl.BlockSpec((tm, d), lambda i: (i + start // tm, 0))]
    elif len(xs) == 1:
        n = n_rows[0]
        x_specs = [_rows(tm, d)]
    else:
        n = sum(n_rows)
        x_specs = [pl.BlockSpec((tm, d), lambda i: (jnp.minimum(i, tiles_first - 1), 0)),
                   pl.BlockSpec((tm, d), lambda i: (jnp.maximum(i - tiles_first, 0), 0))]
    ins = list(xs) + [g.reshape(1, d), wg, wu, wd]
    specs = x_specs + [_resident((1, d)), _layer_weight(wg.shape, layer), _layer_weight(wu.shape, layer),
                       _layer_weight(wd.shape, layer)]
    if g_final is not None:
        ins.append(g_final.reshape(1, d))
        specs.append(_resident((1, d)))
    kern = functools.partial(_ffn_kernel, n_x=len(xs), tiles_first=tiles_first, final_norm=g_final is not None)
    return pl.pallas_call(
        kern,
        grid=(n // tm,),
        in_specs=specs,
        out_specs=_rows(tm, d),
        out_shape=jax.ShapeDtypeStruct((n, d), F32),
        compiler_params=_params(1),
        name="ffn",
    )(*ins)


def _gmlp_kernel(x_ref, g_ref, win_ref, gv_ref, ws_ref, bs_ref, wout_ref, o_ref, v_ref, uv_ref, *, tm):
    half = gv_ref.shape[1]
    gdim = half // A_GROUPS
    width = A_BLOCK_GROUPS * gdim
    x = x_ref[...]
    h = _rms(x, g_ref[...]).astype(BF16)
    ssq = jnp.zeros((tm, 1), F32)
    for blk in range(half // width):
        cols = slice(blk * width, (blk + 1) * width)
        v = jax.nn.gelu(_dot(h, win_ref[:, half + blk * width:half + (blk + 1) * width]))
        ssq = ssq + jnp.sum(v * v, axis=-1, keepdims=True)
        v_ref[:, cols] = v
    inv = lax.rsqrt(ssq / half + EPS)
    acc = x
    for blk in range(half // width):
        cols = slice(blk * width, (blk + 1) * width)
        u = jax.nn.gelu(_dot(h, win_ref[:, cols]))
        v = (v_ref[:, cols] * inv * gv_ref[:, cols]).astype(BF16)
        for c in range(tm // A_CHUNK):
            rows = slice(c * A_CHUNK, (c + 1) * A_CHUNK)
            for g in range(A_BLOCK_GROUPS):
                gc = slice(g * gdim, (g + 1) * gdim)
                sv = _dot(ws_ref[blk * A_BLOCK_GROUPS + g], v[rows, gc]) + bs_ref[blk * A_BLOCK_GROUPS + g]
                uv_ref[blk, rows, gc] = (u[rows, gc] * sv).astype(BF16)
        acc = acc + _dot(uv_ref[blk], wout_ref[cols, :])
    o_ref[...] = acc


def _gmlp(x, g, w_in, g_v, w_sp, b_sp, w_out, *, tm=768):
    n, d = x.shape
    hid = w_in.shape[1]
    half = hid // 2
    assert n % tm == 0 and tm % A_CHUNK == 0
    return pl.pallas_call(
        functools.partial(_gmlp_kernel, tm=tm),
        grid=(n // tm,),
        in_specs=[
            _rows(tm, d), _resident((1, d)), _resident((d, hid)), _resident((1, half)),
            _resident((A_GROUPS, A_CHUNK, A_CHUNK)), _resident((A_GROUPS, A_CHUNK, 1)),
            _resident((half, d)),
        ],
        out_specs=_rows(tm, d),
        out_shape=jax.ShapeDtypeStruct((n, d), F32),
        scratch_shapes=[pltpu.VMEM((tm, half), F32),
                        pltpu.VMEM((A_GROUPS // A_BLOCK_GROUPS, tm, A_BLOCK_GROUPS * (half // A_GROUPS)), BF16)],
        compiler_params=_params(1),
        name="gmlp",
    )(x, g.reshape(1, d), w_in.astype(BF16), g_v.reshape(1, half), w_sp.astype(BF16),
      b_sp.reshape(A_GROUPS, A_CHUNK, 1), w_out.astype(BF16))


def _seq_pos(row0, n_first, s_first, s_second):
    in_first = row0 < n_first
    pos = jnp.where(in_first, row0 % s_first, (row0 - n_first) % s_second)
    return pos, jnp.where(in_first, s_first, s_second)


def _conv_kernel(xp_ref, xc_ref, xn_ref, g_ref, w1_ref, b1_ref, wdw_ref, bdw_ref, gn_ref, w2_ref, b2_ref,
                 o_ref, glu_ref, conv_ref, shift_ref, *, t, n_first, s_first, s_second):
    d = xc_ref.shape[1]
    ext = t + 2 * CONV_HALO
    pos0, seq_len = _seq_pos(pl.program_id(0) * t, n_first, s_first, s_second)
    n_cb = d // LANES
    piece = t // CONV_PIECES
    for k in range(CONV_PIECES):
        xa = xc_ref[k * piece:(k + 1) * piece, :]
        lo, hi = CONV_HALO + k * piece, CONV_HALO + (k + 1) * piece
        if k == 0:
            xa, lo = jnp.concatenate([xp_ref[...], xa], axis=0), 0
        if k == CONV_PIECES - 1:
            xa, hi = jnp.concatenate([xa, xn_ref[...]], axis=0), ext
        h = _rms(xa, g_ref[...]).astype(BF16)
        p = _dot(h, w1_ref[...]) + b1_ref[...]
        glu = p[:, :d] * jax.nn.sigmoid(p[:, d:])
        pos = pos0 - CONV_HALO + lo + lax.broadcasted_iota(jnp.int32, (hi - lo, 1), 0)
        glu = jnp.where((pos >= 0) & (pos < seq_len), glu, 0.0)
        for cb in range(n_cb):
            glu_ref[cb, lo:hi, :] = glu[:, cb * LANES:(cb + 1) * LANES]
    base = CONV_HALO - CONV_PAD
    span = ((CONV_WIDTH - 1) // SUBLANES) * SUBLANES

    def lane_block(cb, carry):
        for b in range(SUBLANES):
            shift_ref[b] = glu_ref[cb, base + b:base + b + t + span, :]
        for rb in range(t // CONV_ROWS):
            acc = jnp.zeros((CONV_ROWS, LANES), F32)
            for j in range(CONV_WIDTH):
                r0 = rb * CONV_ROWS + j - j % SUBLANES
                acc = acc + wdw_ref[cb, j:j + 1, :] * shift_ref[j % SUBLANES, r0:r0 + CONV_ROWS, :]
            conv_ref[cb, rb * CONV_ROWS:(rb + 1) * CONV_ROWS, :] = acc
        return carry

    lax.fori_loop(0, n_cb, lane_block, 0)
    for k in range(CONV_PIECES):
        rows = slice(k * piece, (k + 1) * piece)
        c = jnp.concatenate([conv_ref[cb, rows, :] for cb in range(n_cb)], axis=1) + bdw_ref[...]
        c = _rms(c, gn_ref[...])
        c = (c * jax.nn.sigmoid(c)).astype(BF16)
        o_ref[rows, :] = xc_ref[rows, :] + _dot(c, w2_ref[...]) + b2_ref[...]


def _conformer(x, g, w1, b1, wdw, bdw, gn, w2, b2, *, n_first, s_first, s_second, t=1024):
    n, d = x.shape
    assert n % t == 0 and s_first % t == 0 and s_second % t == 0 and t % CONV_HALO == 0
    hb = t // CONV_HALO
    n_hb = n // CONV_HALO
    n_cb = d // LANES
    kern = functools.partial(_conv_kernel, t=t, n_first=n_first, s_first=s_first, s_second=s_second)
    return pl.pallas_call(
        kern,
        grid=(n // t,),
        in_specs=[
            pl.BlockSpec((CONV_HALO, d), lambda i: (jnp.maximum(i * hb - 1, 0), 0)),
            _rows(t, d),
            pl.BlockSpec((CONV_HALO, d), lambda i: (jnp.minimum((i + 1) * hb, n_hb - 1), 0)),
            _resident((1, d)), _resident((d, 2 * d)), _resident((1, 2 * d)),
            _resident((n_cb, CONV_WIDTH, LANES)), _resident((1, d)), _resident((1, d)),
            _resident((d, d)), _resident((1, d)),
        ],
        out_specs=_rows(t, d),
        out_shape=jax.ShapeDtypeStruct((n, d), F32),
        scratch_shapes=[pltpu.VMEM((n_cb, t + 2 * CONV_HALO, LANES), F32),
                        pltpu.VMEM((n_cb, t, LANES), F32),
                        pltpu.VMEM((SUBLANES, t + (CONV_WIDTH - 1) // SUBLANES * SUBLANES, LANES), F32)],
        compiler_params=_params(1),
        name="conformer",
    )(x, x, x, g.reshape(1, d), w1.astype(BF16), b1.reshape(1, 2 * d),
      wdw.reshape(CONV_WIDTH, d // LANES, LANES).transpose(1, 0, 2), bdw.reshape(1, d),
      gn.reshape(1, d), w2.astype(BF16), b2.reshape(1, d))


def _qkv_kernel(x_ref, g_ref, w_ref, *rest, tm):
    o_refs, h_ref = rest[:-1], rest[-1]
    hn = _rms(x_ref[...], g_ref[...])
    n_cb = h_ref.shape[0]
    for cb in range(n_cb):
        h_ref[cb] = hn[:, cb * LANES:(cb + 1) * LANES]
    gw = B_GROUP_WIDTH
    scale = 1.0 / math.sqrt(B_HEAD_DIM)
    for grp, (o_ref, (_, dil)) in enumerate(zip(o_refs, B_PATTERNS)):
        rows = tm // dil
        if dil == 1:
            h = hn
        else:
            h = jnp.concatenate(
                [jnp.concatenate([h_ref[cb, pl.ds(r, rows, stride=dil), :] for cb in range(n_cb)], axis=1)
                 for r in range(dil)], axis=0)
        qkv = _dot(h.astype(BF16), w_ref[grp])
        for r in range(dil):
            blk = qkv[r * rows:(r + 1) * rows]
            o_ref[r, :, :gw] = (blk[:, :gw] * scale).astype(BF16)
            o_ref[r, :, gw:] = blk[:, gw:].astype(BF16)


def _qkv_proj(x, g, w, *, tm=1024):
    n, d = x.shape
    ng = len(B_PATTERNS)
    gw = B_GROUP_WIDTH
    assert n % tm == 0 and w.shape[1] == 3 * ng * gw
    w_grp = w.astype(BF16).reshape(d, 3, ng, gw).transpose(2, 0, 1, 3).reshape(ng, d, 3 * gw)
    return pl.pallas_call(
        functools.partial(_qkv_kernel, tm=tm),
        grid=(n // tm,),
        in_specs=[_rows(tm, d), _resident((1, d)), _resident((ng, d, 3 * gw))],
        out_specs=[pl.BlockSpec((dil, tm // dil, 3 * gw), lambda i: (0, i, 0)) for _, dil in B_PATTERNS],
        out_shape=[jax.ShapeDtypeStruct((dil, n // dil, 3 * gw), BF16) for _, dil in B_PATTERNS],
        scratch_shapes=[pltpu.VMEM((d // LANES, tm, LANES), F32)],
        compiler_params=_params(1),
        name="qkv_proj",
    )(x, g.reshape(1, d), w_grp)


def _t5_bucket(rel):
    half = REL_BUCKETS // 2
    max_exact = half // 2
    ret = jnp.where(rel > 0, half, 0)
    n = jnp.abs(rel)
    nf = jnp.maximum(n, 1).astype(jnp.float32)
    large = max_exact + (jnp.log(nf / max_exact) / math.log(REL_MAX_DIST / max_exact)
                         * (half - max_exact)).astype(jnp.int32)
    large = jnp.minimum(large, half - 1)
    return ret + jnp.where(n < max_exact, n, large)


def _bias_kernel(table_ref, bucket_ref, rel_ref, o_ref):
    bucket = bucket_ref[...]
    tk = bucket.shape[1]
    in_band = jnp.abs(rel_ref[...]) <= B_W_HALF
    kk = lax.broadcasted_iota(jnp.int32, bucket.shape, 1)
    after_start, before_end = kk >= B_W_HALF, kk < tk - B_W_HALF
    masks = [in_band, in_band & after_start, in_band & before_end, in_band & after_start & before_end]
    for h in range(o_ref.shape[1]):
        acc = jnp.zeros(bucket.shape, F32)
        for b in range(REL_BUCKETS):
            acc = jnp.where(bucket == b, table_ref[b, h], acc)
        for e, mask in enumerate(masks):
            o_ref[e, h] = jnp.where(mask, acc, NEG_INF)


def _band_bias(table, dil):
    t = B_QUERY_BLOCK
    tk = t + 2 * B_W_HALF
    q_idx = jnp.arange(t, dtype=jnp.int32)
    k_idx = jnp.arange(tk, dtype=jnp.int32)
    rel = k_idx[None, :] - B_W_HALF - q_idx[:, None]
    bucket = _t5_bucket(rel * dil)
    nh = table.shape[1]
    return pl.pallas_call(
        _bias_kernel,
        in_specs=[pl.BlockSpec(memory_space=pltpu.SMEM), pl.BlockSpec((t, tk), lambda: (0, 0)),
                  pl.BlockSpec((t, tk), lambda: (0, 0))],
        out_specs=pl.BlockSpec((4, nh, t, tk), lambda: (0, 0, 0, 0)),
        out_shape=jax.ShapeDtypeStruct((4, nh, t, tk), F32),
        name="band_bias",
    )(table, bucket, rel)


def _attn_kernel(q_ref, kp_ref, kc_ref, kn_ref, vp_ref, vc_ref, vn_ref, bias_ref, o_ref, lse_ref,
                 *, t, rows_first, l_first, l_second):
    qb = B_QUERY_BLOCK
    n_qb = t // qb
    pos0, seq_len = _seq_pos(pl.program_id(1) * t, rows_first, l_first, l_second)
    at_start = (pos0 == 0).astype(jnp.int32)
    at_end = (pos0 + t == seq_len).astype(jnp.int32)
    lane = lax.broadcasted_iota(jnp.int32, (1, LANES), 1)
    low = lane < B_HEAD_DIM
    for res in range(q_ref.shape[0]):
        k = jnp.concatenate([kp_ref[res], kc_ref[res], kn_ref[res]], axis=0)
        v = jnp.concatenate([vp_ref[res], vc_ref[res], vn_ref[res]], axis=0)
        q = q_ref[res]
        for j in range(n_qb):
            edge = (at_start if j == 0 else 0) + (2 * at_end if j == n_qb - 1 else 0)
            q_rows = slice(j * qb, (j + 1) * qb)
            k_rows = slice(j * qb, (j + 1) * qb + 2 * B_W_HALF)
            for pair in range(B_GROUP_WIDTH // LANES):
                cols = slice(pair * LANES, (pair + 1) * LANES)
                qp, kp, vp = q[q_rows, cols], k[k_rows, cols], v[k_rows, cols]
                outs, lses = [], []
                for sub in range(2):
                    qh = jnp.where(low if sub == 0 else ~low, qp, jnp.zeros_like(qp))
                    s = lax.dot_general(qh, kp, (((1,), (1,)), ((), ())), preferred_element_type=F32)
                    s = s + bias_ref[edge, 2 * pair + sub]
                    m = jnp.max(s, axis=-1, keepdims=True)
                    p = jnp.exp(s - m)
                    l = jnp.sum(p, axis=-1, keepdims=True)
                    outs.append(_dot(p.astype(BF16), vp) * (1.0 / l))
                    lses.append(m + jnp.log(l))
                o_ref[res, q_rows, cols] = jnp.where(low, outs[0], outs[1])
                lse_ref[res, q_rows, cols] = jnp.where(low, lses[0], lses[1])


def _band_attention(qkv, bias, *, n_first, s_first, s_second):
    dil, rows, _ = qkv.shape
    gw = B_GROUP_WIDTH
    l_first, l_second = s_first // dil, s_second // dil
    t = min(B_MAX_TILE, l_first)
    assert l_first % t == 0 and l_second % t == 0 and t % B_QUERY_BLOCK == 0
    n_res = min(dil, B_MAX_TILE // t)
    assert dil % n_res == 0
    hb = t // B_W_HALF
    n_hb = rows // B_W_HALF

    def cur(which):
        return pl.BlockSpec((n_res, t, gw), lambda r, i: (r, i, which))

    def prev(which):
        return pl.BlockSpec((n_res, B_W_HALF, gw), lambda r, i: (r, jnp.maximum(i * hb - 1, 0), which))

    def nxt(which):
        return pl.BlockSpec((n_res, B_W_HALF, gw), lambda r, i: (r, jnp.minimum((i + 1) * hb, n_hb - 1), which))

    kern = functools.partial(_attn_kernel, t=t, rows_first=n_first // dil, l_first=l_first, l_second=l_second)
    out_spec = pl.BlockSpec((n_res, t, gw), lambda r, i: (r, i, 0))
    return pl.pallas_call(
        kern,
        grid=(dil // n_res, rows // t),
        in_specs=[cur(0), prev(1), cur(1), nxt(1), prev(2), cur(2), nxt(2),
                  pl.BlockSpec(bias.shape, lambda r, i: (0, 0, 0, 0), pipeline_mode=pl.Buffered(1))],
        out_specs=[out_spec, out_spec],
        out_shape=[jax.ShapeDtypeStruct((dil, rows, gw), F32)] * 2,
        compiler_params=_params(2),
        name=f"band_attn_d{dil}",
    )(qkv, qkv, qkv, qkv, qkv, qkv, qkv, bias)


def _combine_kernel(x_ref, o0_ref, o1_ref, o2_ref, l0_ref, l1_ref, l2_ref, w_ref, *rest, with_ffn):
    if with_ffn:
        g_ref, wg_ref, wu_ref, wd_ref, out_ref, o_scr, l_scr, x_scr = rest
    else:
        out_ref, o_scr, l_scr = rest
    n_cb = o_scr.shape[1]
    for grp, (o_ref, l_ref) in enumerate(((o0_ref, l0_ref), (o1_ref, l1_ref), (o2_ref, l2_ref))):
        dil, rows, _ = o_ref.shape
        for r in range(dil):
            for cb in range(n_cb):
                cols = slice(cb * LANES, (cb + 1) * LANES)
                o_scr[grp, cb, pl.ds(r, rows, stride=dil), :] = o_ref[r, :, cols]
                l_scr[grp, cb, pl.ds(r, rows, stride=dil), :] = l_ref[r, :, cols]

    def full(scr, grp):
        return jnp.concatenate([scr[grp, cb] for cb in range(n_cb)], axis=1)

    l0, l1, l2 = full(l_scr, 0), full(l_scr, 1), full(l_scr, 2)
    m = jnp.maximum(jnp.maximum(l0, l1), l2)
    e0, e1, e2 = jnp.exp(l0 - m), jnp.exp(l1 - m), jnp.exp(l2 - m)
    inv = 1.0 / (e0 + e1 + e2)
    acc = x_ref[...]
    for grp, e in enumerate((e0, e1, e2)):
        acc = acc + _dot((full(o_scr, grp) * (e * inv)).astype(BF16), w_ref[grp])
    if not with_ffn:
        out_ref[...] = acc
        return
    x_scr[...] = acc
    sub = x_scr.shape[0] // COMBINE_FFN_SUBTILES
    for s in range(COMBINE_FFN_SUBTILES):
        rows = slice(s * sub, (s + 1) * sub)
        out_ref[rows, :] = _swiglu_residual(x_scr[rows, :], g_ref, wg_ref, wu_ref, wd_ref)


def _combine_proj(x, outs, lses, w_out, ffn=None, *, tm=512):
    n, d = x.shape
    gw = B_GROUP_WIDTH
    ng = len(outs)
    assert n % tm == 0
    grouped = [pl.BlockSpec((o.shape[0], tm // o.shape[0], gw), lambda i: (0, i, 0)) for o in outs]
    ins = [x, *outs, *lses, w_out.astype(BF16).reshape(ng, gw, d)]
    specs = [_rows(tm, d)] + grouped + grouped + [_resident((ng, gw, d))]
    scratch = [pltpu.VMEM((ng, gw // LANES, tm, LANES), F32)] * 2
    if ffn is not None:
        g, wg, wu, wd, layer = ffn
        ins += [g.reshape(1, d), wg, wu, wd]
        specs += [_resident((1, d)), _layer_weight(wg.shape, layer), _layer_weight(wu.shape, layer),
                  _layer_weight(wd.shape, layer)]
        scratch = scratch + [pltpu.VMEM((tm, d), F32)]
    return pl.pallas_call(
        functools.partial(_combine_kernel, with_ffn=ffn is not None),
        grid=(n // tm,),
        in_specs=specs,
        out_specs=_rows(tm, d),
        out_shape=jax.ShapeDtypeStruct((n, d), F32),
        scratch_shapes=scratch,
        compiler_params=_params(1),
        name="attn_combine",
    )(*ins)


def _dilated_attention(x, g, w_qkv, w_out, rel_bias, ffn=None, *, n_first, s_first, s_second):
    qkvs = _qkv_proj(x, g, w_qkv)
    outs, lses = [], []
    for grp, (window, dil) in enumerate(B_PATTERNS):
        assert window // (2 * dil) == B_W_HALF
        table = rel_bias[:, grp * B_HEADS_PER_GROUP:(grp + 1) * B_HEADS_PER_GROUP]
        bias = _band_bias(table, dil)
        o, l = _band_attention(qkvs[grp], bias, n_first=n_first, s_first=s_first, s_second=s_second)
        outs.append(o)
        lses.append(l)
    return _combine_proj(x, outs, lses, w_out, ffn)


def kernel(x_prompt, x_sample, norm_ffn1, ffn1_w_gate, ffn1_w_up, ffn1_w_down, norm_mix, norm_ffn2, ffn2_w_gate, ffn2_w_up, ffn2_w_down, a_w_in, a_g_v, a_w_spatial, a_b_spatial, a_w_out, b_w_qkv, b_w_out, rel_bias, c_w_pw1, c_b_pw1, c_w_dw, c_b_dw, c_g_norm, c_w_pw2, c_b_pw2, norm_final):
    bp, sp, d = x_prompt.shape
    bs, ss, _ = x_sample.shape
    n_first = bp * sp
    depth = norm_ffn1.shape[0]
    seq = dict(n_first=n_first, s_first=sp, s_second=ss)
    w1 = [w.astype(BF16) for w in (ffn1_w_gate, ffn1_w_up, ffn1_w_down)]
    w2 = [w.astype(BF16) for w in (ffn2_w_gate, ffn2_w_up, ffn2_w_down)]
    xs = [x_prompt.reshape(n_first, d), x_sample.reshape(bs * ss, d)]
    for i in range(depth):
        x = _ffn(xs if i == 0 else [x], norm_ffn1[i], *w1, i)
        kind, j = i % 3, i // 3
        if kind == 0:
            x = _gmlp(x, norm_mix[i], a_w_in[j], a_g_v[j], a_w_spatial[j], a_b_spatial[j], a_w_out[j])
        elif kind == 1:
            ffn2 = (norm_ffn2[i], *w2, i) if i < depth - 1 else None
            x = _dilated_attention(x, norm_mix[i], b_w_qkv[j], b_w_out[j], rel_bias, ffn2, **seq)
            if ffn2 is not None:
                continue
        else:
            x = _conformer(x, norm_mix[i], c_w_pw1[j], c_b_pw1[j], c_w_dw[j], c_b_dw[j], c_g_norm[j],
                           c_w_pw2[j], c_b_pw2[j], **seq)
        if i < depth - 1:
            x = _ffn([x], norm_ffn2[i], *w2, i)
        else:
            y_prompt = _ffn([x], norm_ffn2[i], *w2, i, norm_final, row_range=(0, n_first))
            y_sample = _ffn([x], norm_ffn2[i], *w2, i, norm_final, row_range=(n_first, bs * ss))
    return y_prompt.reshape(bp, sp, d), y_sample.reshape(bs, ss, d)
```
